```python
import jax
import jax.numpy as jnp
from jax import lax
import numpy as np

D_MODEL = 1024
BATCH = 8
SEQ = 2048
DEPTH = 2

GRID_W = 64
CTX_LEN = 256
CONV_DIM = D_MODEL
CONV_WIDTH = 31
MLSTM_HEADS = 4
MLSTM_DIM = 2 * D_MODEL
MLSTM_HEAD_DIM = MLSTM_DIM // MLSTM_HEADS
QK_CONV_WIDTH = 5
CHUNK = 128
D_FF = 4 * D_MODEL
N_MOD = 6
EPS = 1e-6
M_INIT = -1e30

IN_SIZES = (CONV_DIM, CONV_DIM, MLSTM_DIM, MLSTM_DIM, MLSTM_DIM, MLSTM_DIM, 4 * MLSTM_HEADS, D_MODEL, D_MODEL)
IN_DIM = sum(IN_SIZES)
IN_SPLITS = tuple(sum(IN_SIZES[:i + 1]) for i in range(len(IN_SIZES) - 1))

kernel_name = 'hybrid_conformer_mlstm_prefix_dit'


def _rmsnorm(x, g):
    xf = x.astype(jnp.float32)
    y = xf * lax.rsqrt(jnp.mean(xf * xf, axis=-1, keepdims=True) + EPS)
    return (y * g.astype(jnp.float32)).astype(x.dtype)


def _layernorm(x, g, b):
    xf = x.astype(jnp.float32)
    mu = jnp.mean(xf, axis=-1, keepdims=True)
    var = jnp.mean(jnp.square(xf - mu), axis=-1, keepdims=True)
    y = (xf - mu) * lax.rsqrt(var + EPS)
    return (y * g.astype(jnp.float32) + b.astype(jnp.float32)).astype(x.dtype)


def _dwconv(x, w):
    k = w.shape[0]
    return lax.conv_general_dilated(x, w[:, None, :].astype(x.dtype), window_strides=(1,),
                                    padding=[(k // 2, k // 2)],
                                    dimension_numbers=('NWC', 'WIO', 'NWC'),
                                    feature_group_count=x.shape[-1])


def _modulation(cvec, w_mod, b_mod):
    m = jax.nn.silu(cvec) @ w_mod + b_mod
    if m.ndim == 2:
        m = m[:, None, :]
    return jnp.split(m, N_MOD, axis=-1)


def _modulate(h, shift, scale):
    return h * (1 + scale) + shift


def _init_state(bsz):
    h, d = MLSTM_HEADS, MLSTM_HEAD_DIM
    return (jnp.zeros((bsz, h, d, d), jnp.float32),
            jnp.zeros((bsz, h, d), jnp.float32),
            jnp.full((bsz, h), M_INIT, jnp.float32))


def _mlstm_scan(q, k, v, ig, fg, state):
    bsz, t, nh, dh = q.shape
    nc = t // CHUNK

    def to_chunks(a):
        a = a.reshape((bsz, nc, CHUNK, nh) + a.shape[3:])
        return jnp.moveaxis(a, (1, 3), (0, 2))

    logf = jax.nn.log_sigmoid(fg)
    xs = (to_chunks(q), to_chunks(k), to_chunks(v), to_chunks(ig), to_chunks(logf))
    causal = jnp.tril(jnp.ones((CHUNK, CHUNK), bool))

    def step(carry, inp):
        c_st, n_st, m_st = carry
        qc, kc, vc, ic, lfc = inp
        b = jnp.cumsum(lfc, axis=-1)
        log_d = jnp.where(causal, b[..., :, None] - b[..., None, :] + ic[..., None, :], -jnp.inf)
        inter = b + m_st[..., None]
        m_t = jnp.maximum(inter, jnp.max(log_d, axis=-1))
        w_intra = jnp.exp(log_d - m_t[..., None])
        w_inter = jnp.exp(inter - m_t)
        s = jnp.einsum('bhtd,bhsd->bhts', qc, kc) * w_intra
        num = (w_inter[..., None] * jnp.einsum('bhtd,bhde->bhte', qc, c_st)
               + jnp.einsum('bhts,bhse->bhte', s, vc))
        den = w_inter * jnp.einsum('bhtd,bhd->bht', qc, n_st) + jnp.sum(s, axis=-1)
        h = num / jnp.maximum(jnp.abs(den), jnp.exp(-m_t))[..., None]
        g = b[..., -1:] - b + ic
        inter_end = b[..., -1] + m_st
        m_new = jnp.maximum(inter_end, jnp.max(g, axis=-1))
        ws = jnp.exp(g - m_new[..., None])
        we = jnp.exp(inter_end - m_new)
        kw = kc * ws[..., None]
        c_new = we[..., None, None] * c_st + jnp.einsum('bhsd,bhse->bhde', kw, vc)
        n_new = we[..., None] * n_st + jnp.sum(kw, axis=2)
        return (c_new, n_new, m_new), h

    state, hs = lax.scan(step, state, xs)
    h = jnp.moveaxis(hs, (0, 2), (1, 3)).reshape(bsz, t, nh, dh)
    return h, state


def _bi_scan(q, k, v, i_f, f_f, i_b, f_b, st_f, st_b):
    h_f, st_f = _mlstm_scan(q, k, v, i_f, f_f, st_f)
    fl = lambda a: jnp.flip(a, axis=1)
    h_b, st_b = _mlstm_scan(fl(q), fl(k), fl(v), fl(i_b), fl(f_b), st_b)
    return h_f + fl(h_b), st_f, st_b


def _mlstm_inputs(p, qk_w, gate_b):
    q = jax.nn.silu(_dwconv(p[2], qk_w[:, :MLSTM_DIM]))
    k = jax.nn.silu(_dwconv(p[3], qk_w[:, MLSTM_DIM:])) * (MLSTM_HEAD_DIM ** -0.5)
    bsz, t, _ = q.shape
    shp = (bsz, t, MLSTM_HEADS, MLSTM_HEAD_DIM)
    g = (p[6] + gate_b).astype(jnp.float32).reshape(bsz, t, 4, MLSTM_HEADS)
    return (q.astype(jnp.float32).reshape(shp), k.astype(jnp.float32).reshape(shp),
            p[4].astype(jnp.float32).reshape(shp),
            g[:, :, 0], g[:, :, 1], g[:, :, 2], g[:, :, 3])


def _mixer_out(p, h, conv_rows, dw_w, ln_g, ln_b, w_conv_out, m_norm_g, w_m_out, w_out):
    u = p[0] * jax.nn.sigmoid(p[1])
    bsz, t, cdim = u.shape
    if conv_rows is None:
        u = _dwconv(u, dw_w)
    else:
        u = _dwconv(u.reshape(bsz * conv_rows, GRID_W, cdim), dw_w).reshape(bsz, t, cdim)
    y_conv = jax.nn.silu(_layernorm(u, ln_g, ln_b)) @ w_conv_out
    hn = h * lax.rsqrt(jnp.mean(h * h, axis=-1, keepdims=True) + EPS)
    hn = hn * m_norm_g.astype(jnp.float32).reshape(MLSTM_HEADS, MLSTM_HEAD_DIM)
    hm = hn.reshape(bsz, t, MLSTM_DIM).astype(p[5].dtype) * jax.nn.sigmoid(p[5])
    y_m = hm @ w_m_out
    y = jax.nn.sigmoid(p[7]) * y_conv + jax.nn.sigmoid(p[8]) * y_m
    return y @ w_out


def _ffn(h, w1, w2):
    return jnp.square(jax.nn.relu(h @ w1)) @ w2


def setup_inputs(seed: int = 0) -> dict:
    key = jax.random.key(seed)
    ks = iter(jax.random.split(key, 32))
    nrm = lambda shape, s: jax.random.normal(next(ks), shape, jnp.float32) * s
    gain = lambda shape: 1.0 + nrm(shape, 0.02)
    i_bias = nrm((DEPTH, 2, MLSTM_HEADS), 0.1)
    f_bias = jnp.linspace(3.0, 6.0, MLSTM_HEADS, dtype=jnp.float32) + nrm((DEPTH, 2, MLSTM_HEADS), 0.1)
    gate_b = jnp.stack([i_bias[:, 0], f_bias[:, 0], i_bias[:, 1], f_bias[:, 1]], axis=1).reshape(DEPTH, 4 * MLSTM_HEADS)
    return {
        'x': nrm((BATCH, SEQ, D_MODEL), 1.0),
        'c': nrm((BATCH, D_MODEL), 1.0),
        'ctx': nrm((BATCH, CTX_LEN, D_MODEL), 1.0),
        'c_ctx': nrm((D_MODEL,), 1.0),
        'w_mod': nrm((DEPTH, D_MODEL, N_MOD * D_MODEL), 0.5 * D_MODEL ** -0.5),
        'b_mod': nrm((DEPTH, N_MOD * D_MODEL), 0.02),
        'norm1_g': gain((DEPTH, D_MODEL)),
        'w_in': nrm((DEPTH, D_MODEL, IN_DIM), D_MODEL ** -0.5),
        'mlstm_gate_b': gate_b,
        'qk_conv_w': nrm((DEPTH, QK_CONV_WIDTH, 2 * MLSTM_DIM), QK_CONV_WIDTH ** -0.5),
        'conv_dw_w': nrm((DEPTH, CONV_WIDTH, CONV_DIM), CONV_WIDTH ** -0.5),
        'conv_ln_g': gain((DEPTH, CONV_DIM)),
        'conv_ln_b': nrm((DEPTH, CONV_DIM), 0.02),
        'w_conv_out': nrm((DEPTH, CONV_DIM, D_MODEL), CONV_DIM ** -0.5),
        'mlstm_norm_g': gain((DEPTH, MLSTM_DIM)),
        'w_mlstm_out': nrm((DEPTH, MLSTM_DIM, D_MODEL), MLSTM_DIM ** -0.5),
        'w_out': nrm((DEPTH, D_MODEL, D_MODEL), D_MODEL ** -0.5),
        'norm2_g': gain((DEPTH, D_MODEL)),
        'w_ff1': nrm((DEPTH, D_MODEL, D_FF), D_MODEL ** -0.5),
        'w_ff2': nrm((DEPTH, D_FF, D_MODEL), D_FF ** -0.5),
        'final_g': gain((D_MODEL,)),
    }


def reference(x, c, ctx, c_ctx, w_mod, b_mod, norm1_g, w_in, mlstm_gate_b, qk_conv_w, conv_dw_w,
              conv_ln_g, conv_ln_b, w_conv_out, mlstm_norm_g, w_mlstm_out, w_out, norm2_g,
              w_ff1, w_ff2, final_g):
    bsz = x.shape[0]
    rows = x.shape[1] // GRID_W
    xl, xc = x, ctx
    for l in range(DEPTH):
        last = l == DEPTH - 1
        sh1_l, sc1_l, ga1_l, sh2_l, sc2_l, ga2_l = _modulation(c, w_mod[l], b_mod[l])
        sh1_c, sc1_c, ga1_c, sh2_c, sc2_c, ga2_c = _modulation(c_ctx, w_mod[l], b_mod[l])
        hl = _modulate(_rmsnorm(xl, norm1_g[l]), sh1_l, sc1_l)
        hc = _modulate(_rmsnorm(xc, norm1_g[l]), sh1_c, sc1_c)
        pl = jnp.split(hl @ w_in[l], IN_SPLITS, axis=-1)
        pc = jnp.split(hc @ w_in[l], IN_SPLITS, axis=-1)
        init = _init_state(bsz)
        h_c, st_f, st_b = _bi_scan(*_mlstm_inputs(pc, qk_conv_w[l], mlstm_gate_b[l]), init, init)
        h_l, _, _ = _bi_scan(*_mlstm_inputs(pl, qk_conv_w[l], mlstm_gate_b[l]), st_f, st_b)
        yl = _mixer_out(pl, h_l, rows, conv_dw_w[l], conv_ln_g[l], conv_ln_b[l], w_conv_out[l],
                        mlstm_norm_g[l], w_mlstm_out[l], w_out[l])
        xl = xl + ga1_l * yl
        xl = xl + ga2_l * _ffn(_modulate(_rmsnorm(xl, norm2_g[l]), sh2_l, sc2_l), w_ff1[l], w_ff2[l])
        if not last:
            yc = _mixer_out(pc, h_c, None, conv_dw_w[l], conv_ln_g[l], conv_ln_b[l], w_conv_out[l],
                            mlstm_norm_g[l], w_mlstm_out[l], w_out[l])
            xc = xc + ga1_c * yc
            xc = xc + ga2_c * _ffn(_modulate(_rmsnorm(xc, norm2_g[l]), sh2_c, sc2_c), w_ff1[l], w_ff2[l])
    return _rmsnorm(xl, final_g)
```

```python
import functools

import jax
import jax.numpy as jnp
from jax import lax
from jax.experimental import pallas as pl
from jax.experimental.pallas import tpu as pltpu

D_MODEL = 1024
GRID_W = 64
CONV_WIDTH = 31
CONV_HALO = 16
HEADS = 4
MLSTM_DIM = 2 * D_MODEL
HEAD_DIM = MLSTM_DIM // HEADS
QK_CONV_WIDTH = 5
CHUNK = 128
D_FF = 4 * D_MODEL
N_MOD = 6
EPS = 1e-6
M_INIT = -1e30
GATE_PAD = 128
P_DIM = 12 * D_MODEL
COL_A, COL_B, COL_GC, COL_GM = 0, 1, 10, 11
COL_Q, COL_K, COL_V, COL_O = 4, 8, 12, 16
MOD_ROWS = 16
VMEM_LIMIT = 56 * 1024 * 1024

BF16 = jnp.bfloat16
F32 = jnp.float32


def _silu(x):
    return x * jax.nn.sigmoid(x)


def _bdot(a, b):
    return jnp.dot(a, b, preferred_element_type=F32)


def _cparams(*sem):
    return pltpu.CompilerParams(dimension_semantics=sem, vmem_limit_bytes=VMEM_LIMIT)


def _mod_kernel(cc_ref, w_ref, b_ref, o_ref):
    s = _silu(cc_ref[...]).astype(BF16)
    o_ref[...] = _bdot(s, w_ref[...].astype(BF16)) + b_ref[...]


def _modulation(cc, w_mod, b_mod):
    depth = w_mod.shape[0]
    tn = D_MODEL
    return pl.pallas_call(
        _mod_kernel,
        grid=(depth, N_MOD * D_MODEL // tn),
        in_specs=[
            pl.BlockSpec((MOD_ROWS, D_MODEL), lambda l, j: (0, 0)),
            pl.BlockSpec((None, D_MODEL, tn), lambda l, j: (l, 0, j)),
            pl.BlockSpec((None, 1, tn), lambda l, j: (l, 0, j)),
        ],
        out_specs=pl.BlockSpec((None, MOD_ROWS, tn), lambda l, j: (l, 0, j)),
        out_shape=jax.ShapeDtypeStruct((depth, MOD_ROWS, N_MOD * D_MODEL), F32),
        compiler_params=_cparams("arbitrary", "arbitrary"),
        name="modulation",
    )(cc, w_mod, b_mod.reshape(depth, 1, N_MOD * D_MODEL))


def _mod_spec(k, row_fn):
    return pl.BlockSpec((None, 1, D_MODEL), lambda *idx: (row_fn(*idx), 0, k))


def _rms_modulate(x, g, shift, scale):
    y = x * lax.rsqrt(jnp.mean(x * x, axis=-1, keepdims=True) + EPS) * g
    return y * (1.0 + scale) + shift


def _in_proj_kernel(x_ref, sh_ref, sc_ref, g_ref, w_ref, wg_ref, gb_ref, p_ref, gate_ref, h_scr):
    @pl.when(pl.program_id(1) == 0)
    def _():
        h = _rms_modulate(x_ref[...], g_ref[...], sh_ref[...], sc_ref[...]).astype(BF16)
        h_scr[...] = h
        gate_ref[...] = _bdot(h, wg_ref[...]) + gb_ref[...]

    p_ref[...] = _bdot(h_scr[...], w_ref[...]).astype(BF16)


def _in_proj(x, mods, g, w_main, w_gate, gate_b, row_fn, tm, tn):
    n = x.shape[0]
    return pl.pallas_call(
        _in_proj_kernel,
        grid=(n // tm, P_DIM // tn),
        in_specs=[
            pl.BlockSpec((tm, D_MODEL), lambda i, j: (i, 0)),
            _mod_spec(0, lambda i, j: row_fn(i)),
            _mod_spec(1, lambda i, j: row_fn(i)),
            pl.BlockSpec((1, D_MODEL), lambda i, j: (0, 0)),
            pl.BlockSpec((D_MODEL, tn), lambda i, j: (0, j)),
            pl.BlockSpec((D_MODEL, GATE_PAD), lambda i, j: (0, 0)),
            pl.BlockSpec((1, GATE_PAD), lambda i, j: (0, 0)),
        ],
        out_specs=[
            pl.BlockSpec((tm, tn), lambda i, j: (i, j)),
            pl.BlockSpec((tm, GATE_PAD), lambda i, j: (i, 0)),
        ],
        out_shape=[
            jax.ShapeDtypeStruct((n, P_DIM), BF16),
            jax.ShapeDtypeStruct((n, GATE_PAD), F32),
        ],
        scratch_shapes=[pltpu.VMEM((tm, D_MODEL), BF16)],
        compiler_params=_cparams("arbitrary", "arbitrary"),
        name="in_proj",
    )(x, mods, mods, g, w_main, w_gate, gate_b)


def _qk_conv_kernel(q_ref, k_ref, wq_ref, wk_ref, qo_ref, ko_ref):
    t = q_ref.shape[0]
    row = lax.broadcasted_iota(jnp.int32, (t, 1), 0)
    half = QK_CONV_WIDTH // 2

    def conv(x_ref, w_ref):
        x = x_ref[...].astype(F32)
        acc = x * w_ref[half:half + 1, :]
        for off in range(-half, half + 1):
            if off == 0:
                continue
            shifted = pltpu.roll(x, (-off) % t, 0)
            valid = (row + off >= 0) & (row + off < t)
            acc = acc + jnp.where(valid, shifted, 0.0) * w_ref[off + half:off + half + 1, :]
        return _silu(acc)

    qo_ref[...] = conv(q_ref, wq_ref).astype(BF16)
    ko_ref[...] = (conv(k_ref, wk_ref) * (HEAD_DIM ** -0.5)).astype(BF16)


def _qk_conv(p, qk_w):
    bsz, t, _ = p.shape
    out = jax.ShapeDtypeStruct((bsz, t, MLSTM_DIM), BF16)
    blk = lambda c0: pl.BlockSpec((None, t, HEAD_DIM), lambda b, c: (b, 0, c0 + c))
    wblk = lambda c0: pl.BlockSpec((QK_CONV_WIDTH, HEAD_DIM), lambda b, c: (0, c0 + c))
    return pl.pallas_call(
        _qk_conv_kernel,
        grid=(bsz, HEADS),
        in_specs=[blk(COL_Q), blk(COL_K), wblk(0), wblk(HEADS)],
        out_specs=[blk(0), blk(0)],
        out_shape=[out, out],
        compiler_params=_cparams("arbitrary", "arbitrary"),
        name="qk_conv",
    )(p, p, qk_w, qk_w)


def _mlstm_chunk(q, k, v, g_row, g_col, c_ref, n_ref, m_ref, reverse):
    gi, gf = (2, 3) if reverse else (0, 1)
    i_row, i_col = g_row[gi:gi + 1, :], g_col[:, gi:gi + 1]
    lf_row = jax.nn.log_sigmoid(g_row[gf:gf + 1, :])
    lf_col = jax.nn.log_sigmoid(g_col[:, gf:gf + 1])
    r = lax.broadcasted_iota(jnp.int32, (CHUNK, CHUNK), 0)
    c = lax.broadcasted_iota(jnp.int32, (CHUNK, CHUNK), 1)
    mask = (c >= r) if reverse else (c <= r)
    mask_t = (r >= c) if reverse else (r <= c)
    b_col = jnp.sum(jnp.where(mask, lf_row, 0.0), axis=1, keepdims=True)
    b_row = jnp.sum(jnp.where(mask_t, lf_col, 0.0), axis=0, keepdims=True)
    b_last = jnp.sum(lf_row, axis=1, keepdims=True)
    m_st = m_ref[...]

    log_d = jnp.where(mask, b_col - b_row + i_row, -jnp.inf)
    inter = b_col + m_st
    m_t = jnp.maximum(inter, jnp.max(log_d, axis=1, keepdims=True))
    w_intra = jnp.exp(log_d - m_t)
    w_inter = jnp.exp(inter - m_t)
    s = lax.dot_general(q, k, (((1,), (1,)), ((), ())), preferred_element_type=F32) * w_intra
    num = w_inter * _bdot(q, c_ref[...].astype(BF16)) + _bdot(s.astype(BF16), v)
    den = (w_inter * jnp.sum(q.astype(F32) * n_ref[...], axis=1, keepdims=True)
           + jnp.sum(s, axis=1, keepdims=True))
    h = num / jnp.maximum(jnp.abs(den), jnp.exp(-m_t))

    g_end = b_last - b_col + i_col
    inter_end = b_last + m_st
    m_new = jnp.maximum(inter_end, jnp.max(g_end, axis=0, keepdims=True))
    ws = jnp.exp(g_end - m_new)
    we = jnp.exp(inter_end - m_new)
    kw = k.astype(F32) * ws
    kv = lax.dot_general(kw.astype(BF16), v, (((0,), (0,)), ((), ())), preferred_element_type=F32)
    c_ref[...] = we * c_ref[...] + kv
    n_ref[...] = we * n_ref[...] + jnp.sum(kw, axis=0, keepdims=True)
    m_ref[...] = m_new
    return h


def _mlstm_kernel(qc_ref, kc_ref, vc_ref, oc_ref, ql_ref, kl_ref, vl_ref, ol_ref,
                  grow_ref, gcol_ref, ng_ref, *rest, ctx_out):
    if ctx_out:
        hl_ref, hc_ref, c_ref, n_ref, m_ref, accl_ref, accc_ref = rest
    else:
        hl_ref, c_ref, n_ref, m_ref, accl_ref = rest
        hc_ref = accc_ref = None
    nc_ctx = qc_ref.shape[0] // CHUNK
    nc_lat = ql_ref.shape[0] // CHUNK

    def init_state():
        c_ref[...] = jnp.zeros_like(c_ref)
        n_ref[...] = jnp.zeros_like(n_ref)
        m_ref[...] = jnp.full_like(m_ref, M_INIT)

    def step(refs, j, gate_base, reverse):
        q_ref, k_ref, v_ref = refs
        rows = pl.ds(pl.multiple_of(j * CHUNK, CHUNK), CHUNK)
        return rows, _mlstm_chunk(q_ref[rows, :], k_ref[rows, :], v_ref[rows, :],
                                  grow_ref[gate_base + j], gcol_ref[gate_base + j],
                                  c_ref, n_ref, m_ref, reverse)

    def finish(h, o_ref, out_ref, rows):
        hn = h * lax.rsqrt(jnp.mean(h * h, axis=-1, keepdims=True) + EPS) * ng_ref[...]
        out_ref[rows, :] = (hn * jax.nn.sigmoid(o_ref[rows, :].astype(F32))).astype(BF16)

    ctx_refs = (qc_ref, kc_ref, vc_ref)
    lat_refs = (ql_ref, kl_ref, vl_ref)

    init_state()

    def fwd_ctx(j, carry):
        rows, h = step(ctx_refs, j, 0, False)
        if ctx_out:
            accc_ref[rows, :] = h
        return carry

    lax.fori_loop(0, nc_ctx, fwd_ctx, 0)

    def fwd_lat(j, carry):
        rows, h = step(lat_refs, j, nc_ctx, False)
        accl_ref[rows, :] = h
        return carry

    lax.fori_loop(0, nc_lat, fwd_lat, 0)

    init_state()

    def bwd_ctx(jj, carry):
        rows, h = step(ctx_refs, nc_ctx - 1 - jj, 0, True)
        if ctx_out:
            finish(accc_ref[rows, :] + h, oc_ref, hc_ref, rows)
        return carry

    lax.fori_loop(0, nc_ctx, bwd_ctx, 0)

    def bwd_lat(jj, carry):
        rows, h = step(lat_refs, nc_lat - 1 - jj, nc_ctx, True)
        finish(accl_ref[rows, :] + h, ol_ref, hl_ref, rows)
        return carry

    lax.fori_loop(0, nc_lat, bwd_lat, 0)


def _mlstm(qc, kc, p_ctx, ql, kl, p_lat, g_row, g_col, norm_g, ctx_out):
    bsz, t_ctx, _ = qc.shape
    t_lat = ql.shape[1]
    nchunks = (t_ctx + t_lat) // CHUNK
    blk = lambda t, c0: pl.BlockSpec((None, t, HEAD_DIM), lambda b, h: (b, 0, c0 + h))
    in_specs = [
        blk(t_ctx, 0), blk(t_ctx, 0), blk(t_ctx, COL_V), blk(t_ctx, COL_O),
        blk(t_lat, 0), blk(t_lat, 0), blk(t_lat, COL_V), blk(t_lat, COL_O),
        pl.BlockSpec((None, None, nchunks, 4, CHUNK), lambda b, h: (b, h, 0, 0, 0)),
        pl.BlockSpec((None, None, nchunks, CHUNK, 4), lambda b, h: (b, h, 0, 0, 0)),
        pl.BlockSpec((1, HEAD_DIM), lambda b, h: (0, h)),
    ]
    out_specs = [blk(t_lat, 0)]
    out_shape = [jax.ShapeDtypeStruct((bsz, t_lat, MLSTM_DIM), BF16)]
    scratch = [pltpu.VMEM((HEAD_DIM, HEAD_DIM), F32), pltpu.VMEM((1, HEAD_DIM), F32),
               pltpu.VMEM((1, 1), F32), pltpu.VMEM((t_lat, HEAD_DIM), F32)]
    if ctx_out:
        out_specs.append(blk(t_ctx, 0))
        out_shape.append(jax.ShapeDtypeStruct((bsz, t_ctx, MLSTM_DIM), BF16))
        scratch.append(pltpu.VMEM((t_ctx, HEAD_DIM), F32))
    return pl.pallas_call(
        functools.partial(_mlstm_kernel, ctx_out=ctx_out),
        grid=(bsz, HEADS),
        in_specs=in_specs,
        out_specs=out_specs,
        out_shape=out_shape,
        scratch_shapes=scratch,
        compiler_params=_cparams("arbitrary", "arbitrary"),
        name="mlstm",
    )(qc, kc, p_ctx, p_ctx, ql, kl, p_lat, p_lat, g_row, g_col, norm_g)


def _mixer_kernel(a_ref, b_ref, gc_ref, gm_ref, hm_ref, x_ref, ga_ref, dw_ref, lng_ref, lnb_ref,
                  wc_ref, wm_ref, wo_ref, o_ref, pad_ref, cv_ref, *, group):
    tm = a_ref.shape[0]
    stride = group + 2 * CONV_HALO
    lane_blk = 256
    u = a_ref[...].astype(F32) * jax.nn.sigmoid(b_ref[...].astype(F32))
    halo = jnp.zeros((CONV_HALO, D_MODEL), F32)
    for g in range(tm // group):
        base = g * stride
        pad_ref[base:base + CONV_HALO, :] = halo
        pad_ref[base + CONV_HALO:base + CONV_HALO + group, :] = u[g * group:(g + 1) * group, :]
        pad_ref[base + CONV_HALO + group:base + stride, :] = halo
    first = CONV_HALO - CONV_WIDTH // 2
    for g in range(tm // group):
        for cb in range(D_MODEL // lane_blk):
            cols = slice(cb * lane_blk, (cb + 1) * lane_blk)
            acc = jnp.zeros((group, lane_blk), F32)
            for j in range(CONV_WIDTH):
                start = g * stride + first + j
                acc = acc + pad_ref[start:start + group, cols] * dw_ref[j:j + 1, cols]
            cv_ref[g * group:(g + 1) * group, cols] = acc
    cv = cv_ref[...]
    mu = jnp.mean(cv, axis=-1, keepdims=True)
    var = jnp.mean(jnp.square(cv - mu), axis=-1, keepdims=True)
    ln = (cv - mu) * lax.rsqrt(var + EPS) * lng_ref[...] + lnb_ref[...]
    y_conv = _bdot(_silu(ln).astype(BF16), wc_ref[...])
    y_m = _bdot(hm_ref[...], wm_ref[...])
    y = (jax.nn.sigmoid(gc_ref[...].astype(F32)) * y_conv
         + jax.nn.sigmoid(gm_ref[...].astype(F32)) * y_m)
    o_ref[...] = x_ref[...] + ga_ref[...] * _bdot(y.astype(BF16), wo_ref[...])


def _mixer_out(p, hm, x, mods, dw_w, ln_g, ln_b, w_conv, w_m, w_out, row_fn, tm, group):
    n = x.shape[0]
    pblk = lambda c: pl.BlockSpec((tm, D_MODEL), lambda i: (i, c))
    const = lambda shape: pl.BlockSpec(shape, lambda i: (0, 0))
    return pl.pallas_call(
        functools.partial(_mixer_kernel, group=group),
        grid=(n // tm,),
        in_specs=[
            pblk(COL_A), pblk(COL_B), pblk(COL_GC), pblk(COL_GM),
            pl.BlockSpec((tm, MLSTM_DIM), lambda i: (i, 0)),
            pl.BlockSpec((tm, D_MODEL), lambda i: (i, 0)),
            _mod_spec(2, row_fn),
            const((CONV_WIDTH, D_MODEL)), const((1, D_MODEL)), const((1, D_MODEL)),
            const((D_MODEL, D_MODEL)), const((MLSTM_DIM, D_MODEL)), const((D_MODEL, D_MODEL)),
        ],
        out_specs=pl.BlockSpec((tm, D_MODEL), lambda i: (i, 0)),
        out_shape=jax.ShapeDtypeStruct((n, D_MODEL), F32),
        scratch_shapes=[
            pltpu.VMEM((tm // group * (group + 2 * CONV_HALO), D_MODEL), F32),
            pltpu.VMEM((tm, D_MODEL), F32),
        ],
        compiler_params=_cparams("arbitrary"),
        name="mixer_out",
    )(p, p, p, p, hm, x, mods, dw_w, ln_g, ln_b, w_conv, w_m, w_out)


def _ffn_kernel(x_ref, sh_ref, sc_ref, ga_ref, g_ref, w1_ref, w2_ref, fg_ref, o_ref, *, final_norm):
    x = x_ref[...]
    h = _rms_modulate(x, g_ref[...], sh_ref[...], sc_ref[...]).astype(BF16)
    ff_blk = D_MODEL
    acc = jnp.zeros(x.shape, F32)
    for cb in range(D_FF // ff_blk):
        cols = slice(cb * ff_blk, (cb + 1) * ff_blk)
        t = jnp.square(jnp.maximum(_bdot(h, w1_ref[:, cols]), 0.0))
        acc = acc + _bdot(t.astype(BF16), w2_ref[cols, :])
    y = x + ga_ref[...] * acc
    if final_norm:
        y = y * lax.rsqrt(jnp.mean(y * y, axis=-1, keepdims=True) + EPS) * fg_ref[...]
    o_ref[...] = y


def _ffn(x, mods, g, w1, w2, final_g, row_fn, tm, final_norm):
    n = x.shape[0]
    const = lambda shape: pl.BlockSpec(shape, lambda i: (0, 0))
    return pl.pallas_call(
        functools.partial(_ffn_kernel, final_norm=final_norm),
        grid=(n // tm,),
        in_specs=[
            pl.BlockSpec((tm, D_MODEL), lambda i: (i, 0)),
            _mod_spec(3, row_fn), _mod_spec(4, row_fn), _mod_spec(5, row_fn),
            const((1, D_MODEL)), const((D_MODEL, D_FF)), const((D_FF, D_MODEL)),
            const((1, D_MODEL)),
        ],
        out_specs=pl.BlockSpec((tm, D_MODEL), lambda i: (i, 0)),
        out_shape=jax.ShapeDtypeStruct((n, D_MODEL), F32),
        compiler_params=_cparams("arbitrary"),
        name="ffn",
    )(x, mods, mods, mods, g, w1, w2, final_g)


def _gate_layouts(g_ctx, g_lat, bsz):
    def per_chunk(g):
        return g[:, :4 * HEADS].reshape(bsz, -1, CHUNK, 4, HEADS)
    g = jnp.concatenate([per_chunk(g_ctx), per_chunk(g_lat)], axis=1)
    return jnp.transpose(g, (0, 4, 1, 3, 2)), jnp.transpose(g, (0, 4, 1, 2, 3))


def kernel(x, c, ctx, c_ctx, w_mod, b_mod, norm1_g, w_in, mlstm_gate_b, qk_conv_w, conv_dw_w,
           conv_ln_g, conv_ln_b, w_conv_out, mlstm_norm_g, w_mlstm_out, w_out, norm2_g,
           w_ff1, w_ff2, final_g):
    bsz, t_lat, _ = x.shape
    t_ctx = ctx.shape[1]
    depth = w_mod.shape[0]
    assert bsz < MOD_ROWS and t_lat % GRID_W == 0

    cc = jnp.zeros((MOD_ROWS, D_MODEL), F32).at[:bsz].set(c).at[bsz].set(c_ctx)
    mod_all = _modulation(cc, w_mod, b_mod)

    n_gate = 4 * HEADS
    g_lo = P_DIM - 2 * D_MODEL
    row2 = lambda a: a.reshape(1, -1)

    tm_proj, tn_proj = 1024, 2048
    tm_mix, tm_ffn = 256, 512
    lat_row = lambda tm: (lambda i: i // (t_lat // tm))
    ctx_row = lambda i: bsz

    xl = x.reshape(bsz * t_lat, D_MODEL)
    xc = ctx.reshape(bsz * t_ctx, D_MODEL)
    for l in range(depth):
        last = l == depth - 1
        mods = mod_all[l].reshape(MOD_ROWS, 1, N_MOD * D_MODEL)
        w_main = jnp.concatenate([w_in[l, :, :g_lo], w_in[l, :, g_lo + n_gate:]], axis=1).astype(BF16)
        w_gate = jnp.pad(w_in[l, :, g_lo:g_lo + n_gate], ((0, 0), (0, GATE_PAD - n_gate))).astype(BF16)
        gate_b = jnp.pad(mlstm_gate_b[l], (0, GATE_PAD - n_gate)).reshape(1, GATE_PAD)
        w_conv, w_m, w_o = (w_conv_out[l].astype(BF16), w_mlstm_out[l].astype(BF16),
                            w_out[l].astype(BF16))
        w1, w2 = w_ff1[l].astype(BF16), w_ff2[l].astype(BF16)

        proj = functools.partial(_in_proj, mods=mods, g=row2(norm1_g[l]), w_main=w_main,
                                 w_gate=w_gate, gate_b=gate_b, tm=tm_proj, tn=tn_proj)
        p_lat, g_lat = proj(xl, row_fn=lat_row(tm_proj))
        p_ctx, g_ctx = proj(xc, row_fn=ctx_row)
        p_lat3 = p_lat.reshape(bsz, t_lat, P_DIM)
        p_ctx3 = p_ctx.reshape(bsz, t_ctx, P_DIM)
        ql, kl = _qk_conv(p_lat3, qk_conv_w[l])
        qc, kc = _qk_conv(p_ctx3, qk_conv_w[l])
        g_row, g_col = _gate_layouts(g_ctx, g_lat, bsz)
        hm = _mlstm(qc, kc, p_ctx3, ql, kl, p_lat3, g_row, g_col, row2(mlstm_norm_g[l]),
                    ctx_out=not last)

        mix = functools.partial(_mixer_out, mods=mods, dw_w=conv_dw_w[l], ln_g=row2(conv_ln_g[l]),
                                ln_b=row2(conv_ln_b[l]), w_conv=w_conv, w_m=w_m, w_out=w_o, tm=tm_mix)
        ffn = functools.partial(_ffn, mods=mods, g=row2(norm2_g[l]), w1=w1, w2=w2,
                                final_g=row2(final_g))
        xl = mix(p_lat, hm[0].reshape(bsz * t_lat, MLSTM_DIM), xl, row_fn=lat_row(tm_mix), group=GRID_W)
        xl = ffn(xl, row_fn=lat_row(tm_ffn), tm=tm_ffn, final_norm=last)
        if not last:
            xc = mix(p_ctx, hm[1].reshape(bsz * t_ctx, MLSTM_DIM), xc, row_fn=ctx_row, group=t_ctx)
            xc = ffn(xc, row_fn=ctx_row, tm=tm_ffn, final_norm=False)
    return xl.reshape(bsz, t_lat, D_MODEL)
```

```python
import functools

import jax
import jax.numpy as jnp
from jax import lax
from jax.experimental import pallas as pl
from jax.experimental.pallas import tpu as pltpu

D_MODEL = 1024
GRID_W = 64
CONV_WIDTH = 31
CONV_HALO = 16
CONV_LANE_BLK = 256
SUBLANES = 8
BF16_TILE_ROWS = 16
HEADS = 4
MLSTM_DIM = 2 * D_MODEL
HEAD_DIM = MLSTM_DIM // HEADS
QK_CONV_WIDTH = 5
CHUNK = 128
D_FF = 4 * D_MODEL
N_MOD = 6
EPS = 1e-6
M_INIT = -1e30
GATE_PAD = 128
P_DIM = 12 * D_MODEL
COL_A, COL_B, COL_GC, COL_GM = 0, 1, 10, 11
COL_Q, COL_K, COL_V, COL_O = 4, 8, 12, 16
MOD_ROWS = 16
VMEM_LIMIT = 56 * 1024 * 1024

BF16 = jnp.bfloat16
F32 = jnp.float32


def _silu(x):
    return x * jax.nn.sigmoid(x)


def _bdot(a, b):
    return jnp.dot(a, b, preferred_element_type=F32)


def _cparams(*sem):
    return pltpu.CompilerParams(dimension_semantics=sem, vmem_limit_bytes=VMEM_LIMIT)


def _mod_kernel(cc_ref, w_ref, b_ref, o_ref):
    s = _silu(cc_ref[...]).astype(BF16)
    o_ref[...] = _bdot(s, w_ref[...].astype(BF16)) + b_ref[...]


def _modulation(cc, w_mod, b_mod):
    depth = w_mod.shape[0]
    tn = D_MODEL
    return pl.pallas_call(
        _mod_kernel,
        grid=(depth, N_MOD * D_MODEL // tn),
        in_specs=[
            pl.BlockSpec((MOD_ROWS, D_MODEL), lambda l, j: (0, 0)),
            pl.BlockSpec((None, D_MODEL, tn), lambda l, j: (l, 0, j)),
            pl.BlockSpec((None, 1, tn), lambda l, j: (l, 0, j)),
        ],
        out_specs=pl.BlockSpec((None, MOD_ROWS, tn), lambda l, j: (l, 0, j)),
        out_shape=jax.ShapeDtypeStruct((depth, MOD_ROWS, N_MOD * D_MODEL), F32),
        compiler_params=_cparams("arbitrary", "arbitrary"),
        name="modulation",
    )(cc, w_mod, b_mod.reshape(depth, 1, N_MOD * D_MODEL))


def _mod_spec(k, row_fn):
    return pl.BlockSpec((None, 1, D_MODEL), lambda *idx: (row_fn(*idx), 0, k))


def _rms_modulate(x, g, shift, scale):
    y = x * lax.rsqrt(jnp.mean(x * x, axis=-1, keepdims=True) + EPS) * g
    return y * (1.0 + scale) + shift


def _in_proj_kernel(x_ref, sh_ref, sc_ref, g_ref, w_ref, wg_ref, gb_ref, p_ref, gate_ref, h_scr):
    @pl.when(pl.program_id(1) == 0)
    def _():
        h = _rms_modulate(x_ref[...], g_ref[...], sh_ref[...], sc_ref[...]).astype(BF16)
        h_scr[...] = h
        gate_ref[...] = _bdot(h, wg_ref[...]) + gb_ref[...]

    p_ref[...] = _bdot(h_scr[...], w_ref[...]).astype(BF16)


def _in_proj(x, mods, g, w_main, w_gate, gate_b, row_fn, tm, tn):
    n = x.shape[0]
    return pl.pallas_call(
        _in_proj_kernel,
        grid=(n // tm, P_DIM // tn),
        in_specs=[
            pl.BlockSpec((tm, D_MODEL), lambda i, j: (i, 0)),
            _mod_spec(0, lambda i, j: row_fn(i)),
            _mod_spec(1, lambda i, j: row_fn(i)),
            pl.BlockSpec((1, D_MODEL), lambda i, j: (0, 0)),
            pl.BlockSpec((D_MODEL, tn), lambda i, j: (0, j)),
            pl.BlockSpec((D_MODEL, GATE_PAD), lambda i, j: (0, 0)),
            pl.BlockSpec((1, GATE_PAD), lambda i, j: (0, 0)),
        ],
        out_specs=[
            pl.BlockSpec((tm, tn), lambda i, j: (i, j)),
            pl.BlockSpec((tm, GATE_PAD), lambda i, j: (i, 0)),
        ],
        out_shape=[
            jax.ShapeDtypeStruct((n, P_DIM), BF16),
            jax.ShapeDtypeStruct((n, GATE_PAD), F32),
        ],
        scratch_shapes=[pltpu.VMEM((tm, D_MODEL), BF16)],
        compiler_params=_cparams("arbitrary", "arbitrary"),
        name="in_proj",
    )(x, mods, mods, g, w_main, w_gate, gate_b)


def _qk_conv_kernel(q_ref, k_ref, wq_ref, wk_ref, qo_ref, ko_ref):
    t = q_ref.shape[0]
    half = QK_CONV_WIDTH // 2
    edge = 2 * BF16_TILE_ROWS

    def taps(x, w_ref, valid_fn):
        n = x.shape[0]
        acc = x * w_ref[half:half + 1, :]
        for off in range(-half, half + 1):
            if off == 0:
                continue
            shifted = pltpu.roll(x, (-off) % n, 0)
            if valid_fn is not None:
                shifted = jnp.where(valid_fn(off), shifted, 0.0)
            acc = acc + shifted * w_ref[off + half:off + half + 1, :]
        return acc

    def conv(x_ref, w_ref, o_ref, scale):
        def out(acc):
            y = _silu(acc)
            return (y if scale is None else y * scale).astype(BF16)

        o_ref[...] = out(taps(x_ref[...].astype(F32), w_ref, None))
        row = lax.broadcasted_iota(jnp.int32, (edge, 1), 0)
        top = taps(x_ref[0:edge, :].astype(F32), w_ref, lambda off: row + off >= 0)
        o_ref[0:BF16_TILE_ROWS, :] = out(top[0:BF16_TILE_ROWS, :])
        bot = taps(x_ref[t - edge:t, :].astype(F32), w_ref, lambda off: row + off < edge)
        o_ref[t - BF16_TILE_ROWS:t, :] = out(bot[edge - BF16_TILE_ROWS:edge, :])

    conv(q_ref, wq_ref, qo_ref, None)
    conv(k_ref, wk_ref, ko_ref, HEAD_DIM ** -0.5)


def _qk_conv(p, qk_w):
    bsz, t, _ = p.shape
    out = jax.ShapeDtypeStruct((bsz, t, MLSTM_DIM), BF16)
    blk = lambda c0: pl.BlockSpec((None, t, HEAD_DIM), lambda b, c: (b, 0, c0 + c))
    wblk = lambda c0: pl.BlockSpec((QK_CONV_WIDTH, HEAD_DIM), lambda b, c: (0, c0 + c))
    return pl.pallas_call(
        _qk_conv_kernel,
        grid=(bsz, HEADS),
        in_specs=[blk(COL_Q), blk(COL_K), wblk(0), wblk(HEADS)],
        out_specs=[blk(0), blk(0)],
        out_shape=[out, out],
        compiler_params=_cparams("arbitrary", "arbitrary"),
        name="qk_conv",
    )(p, p, qk_w, qk_w)


def _mlstm_chunk(q, k, v, g_row, g_col, state, reverse):
    c_ref, c16_ref, n_ref, m_ref = state
    gi, gf = (2, 3) if reverse else (0, 1)
    i_row, i_col = g_row[gi:gi + 1, :], g_col[:, gi:gi + 1]
    lf_row = jax.nn.log_sigmoid(g_row[gf:gf + 1, :])
    lf_col = jax.nn.log_sigmoid(g_col[:, gf:gf + 1])
    r = lax.broadcasted_iota(jnp.int32, (CHUNK, CHUNK), 0)
    c = lax.broadcasted_iota(jnp.int32, (CHUNK, CHUNK), 1)
    mask = (c >= r) if reverse else (c <= r)
    mask_t = (r >= c) if reverse else (r <= c)
    b_col = jnp.sum(jnp.where(mask, lf_row, 0.0), axis=1, keepdims=True)
    b_row = jnp.sum(jnp.where(mask_t, lf_col, 0.0), axis=0, keepdims=True)
    b_last = jnp.sum(lf_row, axis=1, keepdims=True)
    m_st = m_ref[...]

    log_d = jnp.where(mask, b_col - b_row + i_row, -jnp.inf)
    inter = b_col + m_st
    m_t = jnp.maximum(inter, jnp.max(log_d, axis=1, keepdims=True))
    w_intra = jnp.exp(log_d - m_t)
    w_inter = jnp.exp(inter - m_t)
    s = lax.dot_general(q, k, (((1,), (1,)), ((), ())), preferred_element_type=F32) * w_intra
    num = w_inter * _bdot(q, c16_ref[...]) + _bdot(s.astype(BF16), v)
    den = (w_inter * jnp.sum(q.astype(F32) * n_ref[...], axis=1, keepdims=True)
           + jnp.sum(s, axis=1, keepdims=True))
    h = num / jnp.maximum(jnp.abs(den), jnp.exp(-m_t))

    g_end = b_last - b_col + i_col
    inter_end = b_last + m_st
    m_new = jnp.maximum(inter_end, jnp.max(g_end, axis=0, keepdims=True))
    ws = jnp.exp(g_end - m_new)
    we = jnp.exp(inter_end - m_new)
    kw = k.astype(F32) * ws
    kv = lax.dot_general(kw.astype(BF16), v, (((0,), (0,)), ((), ())), preferred_element_type=F32)
    c_new = we * c_ref[...] + kv
    c_ref[...] = c_new
    c16_ref[...] = c_new.astype(BF16)
    n_ref[...] = we * n_ref[...] + jnp.sum(kw, axis=0, keepdims=True)
    m_ref[...] = m_new
    return h


def _mlstm_kernel(qc_ref, kc_ref, vc_ref, oc_ref, ql_ref, kl_ref, vl_ref, ol_ref,
                  grow_ref, gcol_ref, ng_ref, *rest, ctx_out):
    if ctx_out:
        hl_ref, hc_ref = rest[:2]
        scratch = rest[2:]
        accc_ref = scratch[9]
    else:
        hl_ref, hc_ref, accc_ref = rest[0], None, None
        scratch = rest[1:]
    state_f, state_b, accl_ref = scratch[0:4], scratch[4:8], scratch[8]
    nc_ctx = qc_ref.shape[0] // CHUNK
    nc_lat = ql_ref.shape[0] // CHUNK
    assert nc_ctx % 2 == 0 and nc_lat % 2 == 0

    for c_ref, c16_ref, n_ref, m_ref in (state_f, state_b):
        c_ref[...] = jnp.zeros_like(c_ref)
        c16_ref[...] = jnp.zeros_like(c16_ref)
        n_ref[...] = jnp.zeros_like(n_ref)
        m_ref[...] = jnp.full_like(m_ref, M_INIT)

    def finish(h, o_ref, out_ref, rows):
        hn = h * lax.rsqrt(jnp.mean(h * h, axis=-1, keepdims=True) + EPS) * ng_ref[...]
        out_ref[rows, :] = (hn * jax.nn.sigmoid(o_ref[rows, :].astype(F32))).astype(BF16)

    def pair_step(i, n_chunks, refs, gate_base, o_ref, out_ref, acc_ref, second_half):
        q_ref, k_ref, v_ref = refs
        for j, state, reverse in ((i, state_f, False), (n_chunks - 1 - i, state_b, True)):
            if isinstance(j, int):
                rows = pl.ds(j * CHUNK, CHUNK)
            else:
                rows = pl.ds(pl.multiple_of(j * CHUNK, CHUNK), CHUNK)
            h = _mlstm_chunk(q_ref[rows, :], k_ref[rows, :], v_ref[rows, :],
                             grow_ref[gate_base + j], gcol_ref[gate_base + j], state, reverse)
            if out_ref is None:
                continue
            if second_half:
                finish(acc_ref[rows, :] + h, o_ref, out_ref, rows)
            else:
                acc_ref[rows, :] = h

    ctx = ((qc_ref, kc_ref, vc_ref), 0, oc_ref, hc_ref, accc_ref)
    lat = ((ql_ref, kl_ref, vl_ref), nc_ctx, ol_ref, hl_ref, accl_ref)
    for n_chunks, args in ((nc_ctx, ctx), (nc_lat, lat)):
        half = n_chunks // 2

        def first(i, carry, n_chunks=n_chunks, args=args):
            pair_step(i, n_chunks, *args, second_half=False)
            return carry

        def second(i, carry, n_chunks=n_chunks, args=args):
            pair_step(i, n_chunks, *args, second_half=True)
            return carry

        if half == 1:
            first(0, 0)
            second(1, 0)
        else:
            lax.fori_loop(0, half, first, 0)
            lax.fori_loop(half, n_chunks, second, 0)


def _mlstm(qc, kc, p_ctx, ql, kl, p_lat, g_row, g_col, norm_g, ctx_out):
    bsz, t_ctx, _ = qc.shape
    t_lat = ql.shape[1]
    nchunks = (t_ctx + t_lat) // CHUNK
    blk = lambda t, c0: pl.BlockSpec((None, t, HEAD_DIM), lambda b, h: (b, 0, c0 + h))
    in_specs = [
        blk(t_ctx, 0), blk(t_ctx, 0), blk(t_ctx, COL_V), blk(t_ctx, COL_O),
        blk(t_lat, 0), blk(t_lat, 0), blk(t_lat, COL_V), blk(t_lat, COL_O),
        pl.BlockSpec((None, None, nchunks, 4, CHUNK), lambda b, h: (b, h, 0, 0, 0)),
        pl.BlockSpec((None, None, nchunks, CHUNK, 4), lambda b, h: (b, h, 0, 0, 0)),
        pl.BlockSpec((1, HEAD_DIM), lambda b, h: (0, h)),
    ]
    out_specs = [blk(t_lat, 0)]
    out_shape = [jax.ShapeDtypeStruct((bsz, t_lat, MLSTM_DIM), BF16)]
    state = [pltpu.VMEM((HEAD_DIM, HEAD_DIM), F32), pltpu.VMEM((HEAD_DIM, HEAD_DIM), BF16),
             pltpu.VMEM((1, HEAD_DIM), F32), pltpu.VMEM((1, 1), F32)]
    scratch = state + state + [pltpu.VMEM((t_lat, HEAD_DIM), F32)]
    if ctx_out:
        out_specs.append(blk(t_ctx, 0))
        out_shape.append(jax.ShapeDtypeStruct((bsz, t_ctx, MLSTM_DIM), BF16))
        scratch.append(pltpu.VMEM((t_ctx, HEAD_DIM), F32))
    return pl.pallas_call(
        functools.partial(_mlstm_kernel, ctx_out=ctx_out),
        grid=(bsz, HEADS),
        in_specs=in_specs,
        out_specs=out_specs,
        out_shape=out_shape,
        scratch_shapes=scratch,
        compiler_params=_cparams("arbitrary", "arbitrary"),
        name="mlstm",
    )(qc, kc, p_ctx, p_ctx, ql, kl, p_lat, p_lat, g_row, g_col, norm_g)


def _mixer_kernel(a_ref, b_ref, gc_ref, gm_ref, hm_ref, x_ref, ga_ref, dw_ref, lng_ref, lnb_ref,
                  wc_ref, wm_ref, wo_ref, o_ref, pad_ref, rot_ref, cv_ref, *, group):
    tm = a_ref.shape[0]
    stride = group + 2 * CONV_HALO
    rows = pad_ref.shape[0]
    lane_blk = rot_ref.shape[2]
    u = a_ref[...].astype(F32) * jax.nn.sigmoid(b_ref[...].astype(F32))
    halo = jnp.zeros((CONV_HALO, D_MODEL), F32)
    for g in range(tm // group):
        base = g * stride
        pad_ref[base:base + CONV_HALO, :] = halo
        pad_ref[base + CONV_HALO:base + CONV_HALO + group, :] = u[g * group:(g + 1) * group, :]
        pad_ref[base + CONV_HALO + group:base + stride, :] = halo
    first = CONV_HALO - CONV_WIDTH // 2
    for cb in range(D_MODEL // lane_blk):
        cols = slice(cb * lane_blk, (cb + 1) * lane_blk)
        for d in range(1, SUBLANES):
            rot_ref[d - 1, 0:rows - SUBLANES, :] = pad_ref[d:d + rows - SUBLANES, cols]
        for g in range(tm // group):
            acc = jnp.zeros((group, lane_blk), F32)
            for j in range(CONV_WIDTH):
                tile, d = divmod(g * stride + first + j, SUBLANES)
                lo = tile * SUBLANES
                slab = pad_ref[lo:lo + group, cols] if d == 0 else rot_ref[d - 1, lo:lo + group, :]
                acc = acc + slab * dw_ref[j:j + 1, cols]
            cv_ref[g * group:(g + 1) * group, cols] = acc
    cv = cv_ref[...]
    mu = jnp.mean(cv, axis=-1, keepdims=True)
    var = jnp.mean(jnp.square(cv - mu), axis=-1, keepdims=True)
    ln = (cv - mu) * lax.rsqrt(var + EPS) * lng_ref[...] + lnb_ref[...]
    y_conv = _bdot(_silu(ln).astype(BF16), wc_ref[...])
    y_m = _bdot(hm_ref[...], wm_ref[...])
    y = (jax.nn.sigmoid(gc_ref[...].astype(F32)) * y_conv
         + jax.nn.sigmoid(gm_ref[...].astype(F32)) * y_m)
    o_ref[...] = x_ref[...] + ga_ref[...] * _bdot(y.astype(BF16), wo_ref[...])


def _mixer_out(p, hm, x, mods, dw_w, ln_g, ln_b, w_conv, w_m, w_out, row_fn, tm, group):
    n = x.shape[0]
    pad_rows = tm // group * (group + 2 * CONV_HALO)
    pblk = lambda c: pl.BlockSpec((tm, D_MODEL), lambda i: (i, c))
    const = lambda shape: pl.BlockSpec(shape, lambda i: (0, 0))
    return pl.pallas_call(
        functools.partial(_mixer_kernel, group=group),
        grid=(n // tm,),
        in_specs=[
            pblk(COL_A), pblk(COL_B), pblk(COL_GC), pblk(COL_GM),
            pl.BlockSpec((tm, MLSTM_DIM), lambda i: (i, 0)),
            pl.BlockSpec((tm, D_MODEL), lambda i: (i, 0)),
            _mod_spec(2, row_fn),
            const((CONV_WIDTH, D_MODEL)), const((1, D_MODEL)), const((1, D_MODEL)),
            const((D_MODEL, D_MODEL)), const((MLSTM_DIM, D_MODEL)), const((D_MODEL, D_MODEL)),
        ],
        out_specs=pl.BlockSpec((tm, D_MODEL), lambda i: (i, 0)),
        out_shape=jax.ShapeDtypeStruct((n, D_MODEL), F32),
        scratch_shapes=[
            pltpu.VMEM((pad_rows, D_MODEL), F32),
            pltpu.VMEM((SUBLANES - 1, pad_rows, CONV_LANE_BLK), F32),
            pltpu.VMEM((tm, D_MODEL), F32),
        ],
        compiler_params=_cparams("arbitrary"),
        name="mixer_out",
    )(p, p, p, p, hm, x, mods, dw_w, ln_g, ln_b, w_conv, w_m, w_out)


def _ffn_kernel(x_ref, sh_ref, sc_ref, ga_ref, g_ref, w1_ref, w2_ref, fg_ref, o_ref, *, final_norm):
    x = x_ref[...]
    h = _rms_modulate(x, g_ref[...], sh_ref[...], sc_ref[...]).astype(BF16)
    ff_blk = D_MODEL
    acc = jnp.zeros(x.shape, F32)
    for cb in range(D_FF // ff_blk):
        cols = slice(cb * ff_blk, (cb + 1) * ff_blk)
        t = jnp.square(jnp.maximum(_bdot(h, w1_ref[:, cols]), 0.0))
        acc = acc + _bdot(t.astype(BF16), w2_ref[cols, :])
    y = x + ga_ref[...] * acc
    if final_norm:
        y = y * lax.rsqrt(jnp.mean(y * y, axis=-1, keepdims=True) + EPS) * fg_ref[...]
    o_ref[...] = y


def _ffn(x, mods, g, w1, w2, final_g, row_fn, tm, final_norm):
    n = x.shape[0]
    const = lambda shape: pl.BlockSpec(shape, lambda i: (0, 0))
    return pl.pallas_call(
        functools.partial(_ffn_kernel, final_norm=final_norm),
        grid=(n // tm,),
        in_specs=[
            pl.BlockSpec((tm, D_MODEL), lambda i: (i, 0)),
            _mod_spec(3, row_fn), _mod_spec(4, row_fn), _mod_spec(5, row_fn),
            const((1, D_MODEL)), const((D_MODEL, D_FF)), const((D_FF, D_MODEL)),
            const((1, D_MODEL)),
        ],
        out_specs=pl.BlockSpec((tm, D_MODEL), lambda i: (i, 0)),
        out_shape=jax.ShapeDtypeStruct((n, D_MODEL), F32),
        compiler_params=_cparams("arbitrary"),
        name="ffn",
    )(x, mods, mods, mods, g, w1, w2, final_g)


def _gate_layouts(g_ctx, g_lat, bsz):
    def per_chunk(g):
        return g[:, :4 * HEADS].reshape(bsz, -1, CHUNK, 4, HEADS)
    g = jnp.concatenate([per_chunk(g_ctx), per_chunk(g_lat)], axis=1)
    return jnp.transpose(g, (0, 4, 1, 3, 2)), jnp.transpose(g, (0, 4, 1, 2, 3))


def kernel(x, c, ctx, c_ctx, w_mod, b_mod, norm1_g, w_in, mlstm_gate_b, qk_conv_w, conv_dw_w,
           conv_ln_g, conv_ln_b, w_conv_out, mlstm_norm_g, w_mlstm_out, w_out, norm2_g,
           w_ff1, w_ff2, final_g):
    bsz, t_lat, _ = x.shape
    t_ctx = ctx.shape[1]
    depth = w_mod.shape[0]
    assert bsz < MOD_ROWS and t_lat % GRID_W == 0

    cc = jnp.zeros((MOD_ROWS, D_MODEL), F32).at[:bsz].set(c).at[bsz].set(c_ctx)
    mod_all = _modulation(cc, w_mod, b_mod)

    n_gate = 4 * HEADS
    g_lo = P_DIM - 2 * D_MODEL
    row2 = lambda a: a.reshape(1, -1)

    tm_proj, tn_proj = 1024, 2048
    tm_mix, tm_ffn = 256, 512
    lat_row = lambda tm: (lambda i: i // (t_lat // tm))
    ctx_row = lambda i: bsz

    xl = x.reshape(bsz * t_lat, D_MODEL)
    xc = ctx.reshape(bsz * t_ctx, D_MODEL)
    for l in range(depth):
        last = l == depth - 1
        mods = mod_all[l].reshape(MOD_ROWS, 1, N_MOD * D_MODEL)
        w_main = jnp.concatenate([w_in[l, :, :g_lo], w_in[l, :, g_lo + n_gate:]], axis=1).astype(BF16)
        w_gate = jnp.pad(w_in[l, :, g_lo:g_lo + n_gate], ((0, 0), (0, GATE_PAD - n_gate))).astype(BF16)
        gate_b = jnp.pad(mlstm_gate_b[l], (0, GATE_PAD - n_gate)).reshape(1, GATE_PAD)
        w_conv, w_m, w_o = (w_conv_out[l].astype(BF16), w_mlstm_out[l].astype(BF16),
                            w_out[l].astype(BF16))
        w1, w2 = w_ff1[l].astype(BF16), w_ff2[l].astype(BF16)

        proj = functools.partial(_in_proj, mods=mods, g=row2(norm1_g[l]), w_main=w_main,
                                 w_gate=w_gate, gate_b=gate_b, tm=tm_proj, tn=tn_proj)
        p_lat, g_lat = proj(xl, row_fn=lat_row(tm_proj))
        p_ctx, g_ctx = proj(xc, row_fn=ctx_row)
        p_lat3 = p_lat.reshape(bsz, t_lat, P_DIM)
        p_ctx3 = p_ctx.reshape(bsz, t_ctx, P_DIM)
        ql, kl = _qk_conv(p_lat3, qk_conv_w[l])
        qc, kc = _qk_conv(p_ctx3, qk_conv_w[l])
        g_row, g_col = _gate_layouts(g_ctx, g_lat, bsz)
        hm = _mlstm(qc, kc, p_ctx3, ql, kl, p_lat3, g_row, g_col, row2(mlstm_norm_g[l]),
                    ctx_out=not last)

        mix = functools.partial(_mixer_out, mods=mods, dw_w=conv_dw_w[l], ln_g=row2(conv_ln_g[l]),
                                ln_b=row2(conv_ln_b[l]), w_conv=w_conv, w_m=w_m, w_out=w_o, tm=tm_mix)
        ffn = functools.partial(_ffn, mods=mods, g=row2(norm2_g[l]), w1=w1, w2=w2,
                                final_g=row2(final_g))
        xl = mix(p_lat, hm[0].reshape(bsz * t_lat, MLSTM_DIM), xl, row_fn=lat_row(tm_mix), group=GRID_W)
        xl = ffn(xl, row_fn=lat_row(tm_ffn), tm=tm_ffn, final_norm=last)
        if not last:
            xc = mix(p_ctx, hm[1].reshape(bsz * t_ctx, MLSTM_DIM), xc, row_fn=ctx_row, group=t_ctx)
            xc = ffn(xc, row_fn=ctx_row, tm=tm_ffn, final_norm=False)
    return xl.reshape(bsz, t_lat, D_MODEL)
```

```python
import functools

import jax
import jax.numpy as jnp
from jax import lax
from jax.experimental import pallas as pl
from jax.experimental.pallas import tpu as pltpu

D_MODEL = 1024
GRID_W = 64
CONV_WIDTH = 31
CONV_HALO = 16
CONV_LANE_BLK = 256
SUBLANES = 8
BF16_TILE_ROWS = 16
HEADS = 4
MLSTM_DIM = 2 * D_MODEL
HEAD_DIM = MLSTM_DIM // HEADS
QK_CONV_WIDTH = 5
CHUNK = 256
D_FF = 4 * D_MODEL
N_MOD = 6
EPS = 1e-6
M_INIT = -1e30
N_GATE = 4
LANES = 128
GATE_COLS = HEADS * LANES
GATE_ROWS = HEADS * SUBLANES
W_BLK = 2 * D_MODEL
N_MAIN_BLKS = 5
P_DIM = 12 * D_MODEL
COL_A, COL_B, COL_GC, COL_GM = 0, 1, 10, 11
COL_Q, COL_K, COL_V, COL_O = 4, 8, 12, 16
MOD_ROWS = 16
VMEM_LIMIT = 56 * 1024 * 1024

BF16 = jnp.bfloat16
F32 = jnp.float32


def _silu(x):
    return x * jax.nn.sigmoid(x)


def _bdot(a, b):
    return jnp.dot(a, b, preferred_element_type=F32)


def _cparams(*sem):
    return pltpu.CompilerParams(dimension_semantics=sem, vmem_limit_bytes=VMEM_LIMIT)


def _mod_kernel(cc_ref, w_ref, b_ref, o_ref):
    s = _silu(cc_ref[...]).astype(BF16)
    o_ref[...] = _bdot(s, w_ref[...].astype(BF16)) + b_ref[...]


def _modulation(cc, w_mod, b_mod):
    depth = w_mod.shape[0]
    tn = D_MODEL
    return pl.pallas_call(
        _mod_kernel,
        grid=(depth, N_MOD * D_MODEL // tn),
        in_specs=[
            pl.BlockSpec((MOD_ROWS, D_MODEL), lambda l, j: (0, 0)),
            pl.BlockSpec((None, D_MODEL, tn), lambda l, j: (l, 0, j)),
            pl.BlockSpec((None, 1, tn), lambda l, j: (l, 0, j)),
        ],
        out_specs=pl.BlockSpec((None, MOD_ROWS, tn), lambda l, j: (l, 0, j)),
        out_shape=jax.ShapeDtypeStruct((depth, MOD_ROWS, N_MOD * D_MODEL), F32),
        compiler_params=_cparams("arbitrary", "arbitrary"),
        name="modulation",
    )(cc, w_mod, b_mod.reshape(depth, 1, N_MOD * D_MODEL))


def _mod_spec(k, row_fn):
    return pl.BlockSpec((None, 1, D_MODEL), lambda *idx: (row_fn(*idx), 0, k))


def _rms_modulate(x, g, shift, scale):
    y = x * lax.rsqrt(jnp.mean(x * x, axis=-1, keepdims=True) + EPS) * g
    return y * (1.0 + scale) + shift


def _in_proj_kernel(x_ref, sh_ref, sc_ref, g_ref, w_ref, wt_ref, wgc_ref, wgr_ref, gbc_ref, gbr_ref,
                    p_ref, gcol_ref, grow_ref, h_scr):
    j = pl.program_id(1)

    @pl.when(j == 0)
    def _():
        h = _rms_modulate(x_ref[...], g_ref[...], sh_ref[...], sc_ref[...]).astype(BF16)
        h_scr[...] = h
        gcol_ref[...] = _bdot(h, wgc_ref[...]) + gbc_ref[...]
        grow_ref[...] = lax.dot_general(wgr_ref[...], h, (((1,), (1,)), ((), ())),
                                        preferred_element_type=F32) + gbr_ref[...]

    @pl.when(j < N_MAIN_BLKS)
    def _():
        p_ref[...] = _bdot(h_scr[...], w_ref[...].astype(BF16)).astype(BF16)

    @pl.when(j == N_MAIN_BLKS)
    def _():
        p_ref[...] = _bdot(h_scr[...], wt_ref[...]).astype(BF16)


def _in_proj(x, mods, g, w_in, layer, w_tail, wg_col, wg_row, gb_col, gb_row, row_fn, tm):
    n = x.shape[0]
    const = lambda shape: pl.BlockSpec(shape, lambda i, j: (0, 0), pipeline_mode=pl.Buffered(1))
    return pl.pallas_call(
        _in_proj_kernel,
        grid=(n // tm, N_MAIN_BLKS + 1),
        in_specs=[
            pl.BlockSpec((tm, D_MODEL), lambda i, j: (i, 0)),
            _mod_spec(0, lambda i, j: row_fn(i)),
            _mod_spec(1, lambda i, j: row_fn(i)),
            const((1, D_MODEL)),
            pl.BlockSpec((None, D_MODEL, W_BLK),
                         lambda i, j: (layer, 0, jnp.minimum(j, N_MAIN_BLKS - 1))),
            const((D_MODEL, W_BLK)),
            const((D_MODEL, GATE_COLS)), const((GATE_ROWS, D_MODEL)),
            const((1, GATE_COLS)), const((GATE_ROWS, 1)),
        ],
        out_specs=[
            pl.BlockSpec((tm, W_BLK), lambda i, j: (i, j)),
            pl.BlockSpec((tm, GATE_COLS), lambda i, j: (i, 0)),
            pl.BlockSpec((GATE_ROWS, tm), lambda i, j: (0, i)),
        ],
        out_shape=[
            jax.ShapeDtypeStruct((n, P_DIM), BF16),
            jax.ShapeDtypeStruct((n, GATE_COLS), F32),
            jax.ShapeDtypeStruct((GATE_ROWS, n), F32),
        ],
        scratch_shapes=[pltpu.VMEM((tm, D_MODEL), BF16)],
        compiler_params=_cparams("arbitrary", "arbitrary"),
        name="in_proj",
    )(x, mods, mods, g, w_in, w_tail, wg_col, wg_row, gb_col, gb_row)


def _qk_conv_kernel(q_ref, k_ref, wq_ref, wk_ref, qo_ref, ko_ref):
    t = q_ref.shape[0]
    half = QK_CONV_WIDTH // 2
    edge = 2 * BF16_TILE_ROWS

    def taps(x, w_ref, valid_fn):
        n = x.shape[0]
        acc = x * w_ref[half:half + 1, :]
        for off in range(-half, half + 1):
            if off == 0:
                continue
            shifted = pltpu.roll(x, (-off) % n, 0)
            if valid_fn is not None:
                shifted = jnp.where(valid_fn(off), shifted, 0.0)
            acc = acc + shifted * w_ref[off + half:off + half + 1, :]
        return acc

    def conv(x_ref, w_ref, o_ref, scale):
        def out(acc):
            y = _silu(acc)
            return (y if scale is None else y * scale).astype(BF16)

        o_ref[...] = out(taps(x_ref[...].astype(F32), w_ref, None))
        row = lax.broadcasted_iota(jnp.int32, (edge, 1), 0)
        top = taps(x_ref[0:edge, :].astype(F32), w_ref, lambda off: row + off >= 0)
        o_ref[0:BF16_TILE_ROWS, :] = out(top[0:BF16_TILE_ROWS, :])
        bot = taps(x_ref[t - edge:t, :].astype(F32), w_ref, lambda off: row + off < edge)
        o_ref[t - BF16_TILE_ROWS:t, :] = out(bot[edge - BF16_TILE_ROWS:edge, :])

    conv(q_ref, wq_ref, qo_ref, None)
    conv(k_ref, wk_ref, ko_ref, HEAD_DIM ** -0.5)


def _qk_conv(p, qk_w):
    bsz, t, _ = p.shape
    out = jax.ShapeDtypeStruct((bsz, t, MLSTM_DIM), BF16)
    blk = lambda c0: pl.BlockSpec((None, t, HEAD_DIM), lambda b, c: (b, 0, c0 + c))
    wblk = lambda c0: pl.BlockSpec((QK_CONV_WIDTH, HEAD_DIM), lambda b, c: (0, c0 + c))
    return pl.pallas_call(
        _qk_conv_kernel,
        grid=(bsz, HEADS),
        in_specs=[blk(COL_Q), blk(COL_K), wblk(0), wblk(HEADS)],
        out_specs=[blk(0), blk(0)],
        out_shape=[out, out],
        compiler_params=_cparams("arbitrary", "arbitrary"),
        name="qk_conv",
    )(p, p, qk_w, qk_w)


def _mlstm_chunk(q, k, v, g_row, g_col, state, reverse):
    c_ref, c16_ref, n_ref, m_ref = state
    gi, gf = (2, 3) if reverse else (0, 1)
    i_row, i_col = g_row[gi:gi + 1, :], g_col[:, gi:gi + 1]
    lf_row = jax.nn.log_sigmoid(g_row[gf:gf + 1, :])
    lf_col = jax.nn.log_sigmoid(g_col[:, gf:gf + 1])
    r = lax.broadcasted_iota(jnp.int32, (CHUNK, CHUNK), 0)
    c = lax.broadcasted_iota(jnp.int32, (CHUNK, CHUNK), 1)
    mask = (c >= r) if reverse else (c <= r)
    mask_t = (r >= c) if reverse else (r <= c)
    b_col = jnp.sum(jnp.where(mask, lf_row, 0.0), axis=1, keepdims=True)
    b_row = jnp.sum(jnp.where(mask_t, lf_col, 0.0), axis=0, keepdims=True)
    b_last = jnp.sum(lf_row, axis=1, keepdims=True)
    m_st = m_ref[...]

    log_d = jnp.where(mask, b_col - b_row + i_row, -jnp.inf)
    inter = b_col + m_st
    m_t = jnp.maximum(inter, jnp.max(log_d, axis=1, keepdims=True))
    w_intra = jnp.exp(log_d - m_t)
    w_inter = jnp.exp(inter - m_t)
    s = lax.dot_general(q, k, (((1,), (1,)), ((), ())), preferred_element_type=F32) * w_intra
    num = w_inter * _bdot(q, c16_ref[...]) + _bdot(s.astype(BF16), v)
    n16 = jnp.broadcast_to(n_ref[...], (SUBLANES, HEAD_DIM)).astype(BF16)
    qn = lax.dot_general(q, n16, (((1,), (1,)), ((), ())), preferred_element_type=F32)[:, 0:1]
    den = w_inter * qn + jnp.sum(s, axis=1, keepdims=True)
    h = num / jnp.maximum(jnp.abs(den), jnp.exp(-m_t))

    g_end = b_last - b_col + i_col
    inter_end = b_last + m_st
    m_new = jnp.maximum(inter_end, jnp.max(g_end, axis=0, keepdims=True))
    ws = jnp.exp(g_end - m_new)
    we = jnp.exp(inter_end - m_new)
    kw = k.astype(F32) * ws
    kv = lax.dot_general(kw.astype(BF16), v, (((0,), (0,)), ((), ())), preferred_element_type=F32)
    c_new = we * c_ref[...] + kv
    c_ref[...] = c_new
    c16_ref[...] = c_new.astype(BF16)
    n_ref[...] = we * n_ref[...] + jnp.sum(kw, axis=0, keepdims=True)
    m_ref[...] = m_new
    return h


def _mlstm_kernel(qc_ref, kc_ref, vc_ref, oc_ref, growc_ref, gcolc_ref,
                  ql_ref, kl_ref, vl_ref, ol_ref, growl_ref, gcoll_ref, ng_ref, *rest, ctx_out):
    if ctx_out:
        hl_ref, hc_ref = rest[:2]
        scratch = rest[2:]
        accc_ref = scratch[9]
    else:
        hl_ref, hc_ref, accc_ref = rest[0], None, None
        scratch = rest[1:]
    state_f, state_b, accl_ref = scratch[0:4], scratch[4:8], scratch[8]
    nc_ctx = qc_ref.shape[0] // CHUNK
    nc_lat = ql_ref.shape[0] // CHUNK
    assert (nc_ctx == 1 or nc_ctx % 2 == 0) and nc_lat % 2 == 0

    for c_ref, c16_ref, n_ref, m_ref in (state_f, state_b):
        c_ref[...] = jnp.zeros_like(c_ref)
        c16_ref[...] = jnp.zeros_like(c16_ref)
        n_ref[...] = jnp.zeros_like(n_ref)
        m_ref[...] = jnp.full_like(m_ref, M_INIT)

    def finish(h, o_ref, out_ref, rows):
        hn = h * lax.rsqrt(jnp.mean(h * h, axis=-1, keepdims=True) + EPS) * ng_ref[...]
        out_ref[rows, :] = (hn * jax.nn.sigmoid(o_ref[rows, :].astype(F32))).astype(BF16)

    def pair_step(i, n_chunks, refs, o_ref, out_ref, acc_ref, closing):
        q_ref, k_ref, v_ref, grow_ref, gcol_ref = refs
        steps = ((i, state_f, False, closing[0]), (n_chunks - 1 - i, state_b, True, closing[1]))
        for j, state, reverse, close in steps:
            if isinstance(j, int):
                rows = pl.ds(j * CHUNK, CHUNK)
            else:
                rows = pl.ds(pl.multiple_of(j * CHUNK, CHUNK), CHUNK)
            h = _mlstm_chunk(q_ref[rows, :], k_ref[rows, :], v_ref[rows, :],
                             grow_ref[j], gcol_ref[rows, :], state, reverse)
            if out_ref is None:
                continue
            if close:
                finish(acc_ref[rows, :] + h, o_ref, out_ref, rows)
            else:
                acc_ref[rows, :] = h

    ctx = ((qc_ref, kc_ref, vc_ref, growc_ref, gcolc_ref), oc_ref, hc_ref, accc_ref)
    lat = ((ql_ref, kl_ref, vl_ref, growl_ref, gcoll_ref), ol_ref, hl_ref, accl_ref)
    for n_chunks, args in ((nc_ctx, ctx), (nc_lat, lat)):
        half = n_chunks // 2

        def opening(i, carry, n_chunks=n_chunks, args=args):
            pair_step(i, n_chunks, *args, closing=(False, False))
            return carry

        def closing(i, carry, n_chunks=n_chunks, args=args):
            pair_step(i, n_chunks, *args, closing=(True, True))
            return carry

        if n_chunks == 1:
            pair_step(0, 1, *args, closing=(False, True))
        elif half == 1:
            opening(0, 0)
            closing(1, 0)
        else:
            lax.fori_loop(0, half, opening, 0)
            lax.fori_loop(half, n_chunks, closing, 0)


def _mlstm(qc, kc, p_ctx, gates_ctx, ql, kl, p_lat, gates_lat, norm_g, ctx_out):
    bsz, t_ctx, _ = qc.shape
    t_lat = ql.shape[1]
    blk = lambda t, c0: pl.BlockSpec((None, t, HEAD_DIM), lambda b, h: (b, 0, c0 + h))

    def stream_specs(t):
        return [
            blk(t, 0), blk(t, 0), blk(t, COL_V), blk(t, COL_O),
            pl.BlockSpec((None, t // CHUNK, SUBLANES, CHUNK), lambda b, h: (h, b, 0, 0)),
            pl.BlockSpec((None, t, LANES), lambda b, h: (b, 0, h)),
        ]

    in_specs = (stream_specs(t_ctx) + stream_specs(t_lat)
                + [pl.BlockSpec((1, HEAD_DIM), lambda b, h: (0, h))])
    out_specs = [blk(t_lat, 0)]
    out_shape = [jax.ShapeDtypeStruct((bsz, t_lat, MLSTM_DIM), BF16)]
    state = [pltpu.VMEM((HEAD_DIM, HEAD_DIM), F32), pltpu.VMEM((HEAD_DIM, HEAD_DIM), BF16),
             pltpu.VMEM((1, HEAD_DIM), F32), pltpu.VMEM((1, 1), F32)]
    scratch = state + state + [pltpu.VMEM((t_lat, HEAD_DIM), F32)]
    if ctx_out:
        out_specs.append(blk(t_ctx, 0))
        out_shape.append(jax.ShapeDtypeStruct((bsz, t_ctx, MLSTM_DIM), BF16))
        scratch.append(pltpu.VMEM((t_ctx, HEAD_DIM), F32))
    return pl.pallas_call(
        functools.partial(_mlstm_kernel, ctx_out=ctx_out),
        grid=(bsz, HEADS),
        in_specs=in_specs,
        out_specs=out_specs,
        out_shape=out_shape,
        scratch_shapes=scratch,
        compiler_params=_cparams("arbitrary", "arbitrary"),
        name="mlstm",
    )(qc, kc, p_ctx, p_ctx, *gates_ctx, ql, kl, p_lat, p_lat, *gates_lat, norm_g)


def _mixer_kernel(a_ref, b_ref, gc_ref, gm_ref, hm_ref, x_ref, ga_ref, dw_ref, lng_ref, lnb_ref,
                  wc_ref, wm_ref, wo_ref, o_ref, pad_ref, rot_ref, cv_ref, *, group):
    tm = a_ref.shape[0]
    stride = group + 2 * CONV_HALO
    rows = pad_ref.shape[0]
    lane_blk = rot_ref.shape[2]
    u = a_ref[...].astype(F32) * jax.nn.sigmoid(b_ref[...].astype(F32))
    halo = jnp.zeros((CONV_HALO, D_MODEL), F32)
    for g in range(tm // group):
        base = g * stride
        pad_ref[base:base + CONV_HALO, :] = halo
        pad_ref[base + CONV_HALO:base + CONV_HALO + group, :] = u[g * group:(g + 1) * group, :]
        pad_ref[base + CONV_HALO + group:base + stride, :] = halo
    first = CONV_HALO - CONV_WIDTH // 2
    for cb in range(D_MODEL // lane_blk):
        cols = slice(cb * lane_blk, (cb + 1) * lane_blk)
        for d in range(1, SUBLANES):
            rot_ref[d - 1, 0:rows - SUBLANES, :] = pad_ref[d:d + rows - SUBLANES, cols]
        for g in range(tm // group):
            acc = jnp.zeros((group, lane_blk), F32)
            for j in range(CONV_WIDTH):
                tile, d = divmod(g * stride + first + j, SUBLANES)
                lo = tile * SUBLANES
                slab = pad_ref[lo:lo + group, cols] if d == 0 else rot_ref[d - 1, lo:lo + group, :]
                acc = acc + slab * dw_ref[j:j + 1, cols]
            cv_ref[g * group:(g + 1) * group, cols] = acc
    cv = cv_ref[...]
    mu = jnp.mean(cv, axis=-1, keepdims=True)
    var = jnp.mean(jnp.square(cv - mu), axis=-1, keepdims=True)
    ln = (cv - mu) * lax.rsqrt(var + EPS) * lng_ref[...] + lnb_ref[...]
    y_conv = _bdot(_silu(ln).astype(BF16), wc_ref[...])
    y_m = _bdot(hm_ref[...], wm_ref[...])
    y = (jax.nn.sigmoid(gc_ref[...].astype(F32)) * y_conv
         + jax.nn.sigmoid(gm_ref[...].astype(F32)) * y_m)
    o_ref[...] = x_ref[...] + ga_ref[...] * _bdot(y.astype(BF16), wo_ref[...])


def _mixer_out(p, hm, x, mods, dw_w, ln_g, ln_b, w_conv, w_m, w_out, row_fn, tm, group):
    n = x.shape[0]
    pad_rows = tm // group * (group + 2 * CONV_HALO)
    pblk = lambda c: pl.BlockSpec((tm, D_MODEL), lambda i: (i, c))
    const = lambda shape: pl.BlockSpec(shape, lambda i: (0, 0))
    return pl.pallas_call(
        functools.partial(_mixer_kernel, group=group),
        grid=(n // tm,),
        in_specs=[
            pblk(COL_A), pblk(COL_B), pblk(COL_GC), pblk(COL_GM),
            pl.BlockSpec((tm, MLSTM_DIM), lambda i: (i, 0)),
            pl.BlockSpec((tm, D_MODEL), lambda i: (i, 0)),
            _mod_spec(2, row_fn),
            const((CONV_WIDTH, D_MODEL)), const((1, D_MODEL)), const((1, D_MODEL)),
            const((D_MODEL, D_MODEL)), const((MLSTM_DIM, D_MODEL)), const((D_MODEL, D_MODEL)),
        ],
        out_specs=pl.BlockSpec((tm, D_MODEL), lambda i: (i, 0)),
        out_shape=jax.ShapeDtypeStruct((n, D_MODEL), F32),
        scratch_shapes=[
            pltpu.VMEM((pad_rows, D_MODEL), F32),
            pltpu.VMEM((SUBLANES - 1, pad_rows, CONV_LANE_BLK), F32),
            pltpu.VMEM((tm, D_MODEL), F32),
        ],
        compiler_params=_cparams("arbitrary"),
        name="mixer_out",
    )(p, p, p, p, hm, x, mods, dw_w, ln_g, ln_b, w_conv, w_m, w_out)


def _ffn_kernel(x_ref, sh_ref, sc_ref, ga_ref, g_ref, w1_ref, w2_ref, fg_ref, o_ref, *, final_norm):
    x = x_ref[...]
    h = _rms_modulate(x, g_ref[...], sh_ref[...], sc_ref[...]).astype(BF16)
    ff_blk = D_MODEL
    acc = jnp.zeros(x.shape, F32)
    for cb in range(D_FF // ff_blk):
        cols = slice(cb * ff_blk, (cb + 1) * ff_blk)
        t = jnp.square(jnp.maximum(_bdot(h, w1_ref[:, cols]), 0.0))
        acc = acc + _bdot(t.astype(BF16), w2_ref[cols, :])
    y = x + ga_ref[...] * acc
    if final_norm:
        y = y * lax.rsqrt(jnp.mean(y * y, axis=-1, keepdims=True) + EPS) * fg_ref[...]
    o_ref[...] = y


def _ffn(x, mods, g, w1, w2, final_g, row_fn, tm, final_norm):
    n = x.shape[0]
    const = lambda shape: pl.BlockSpec(shape, lambda i: (0, 0))
    return pl.pallas_call(
        functools.partial(_ffn_kernel, final_norm=final_norm),
        grid=(n // tm,),
        in_specs=[
            pl.BlockSpec((tm, D_MODEL), lambda i: (i, 0)),
            _mod_spec(3, row_fn), _mod_spec(4, row_fn), _mod_spec(5, row_fn),
            const((1, D_MODEL)), const((D_MODEL, D_FF)), const((D_FF, D_MODEL)),
            const((1, D_MODEL)),
        ],
        out_specs=pl.BlockSpec((tm, D_MODEL), lambda i: (i, 0)),
        out_shape=jax.ShapeDtypeStruct((n, D_MODEL), F32),
        compiler_params=_cparams("arbitrary"),
        name="ffn",
    )(x, mods, mods, mods, g, w1, w2, final_g)


def _gate_weights(w_g, b_g):
    w = w_g.reshape(D_MODEL, N_GATE, HEADS).transpose(0, 2, 1)
    b = b_g.reshape(N_GATE, HEADS).T
    pad_to = lambda a, width: jnp.pad(a, [(0, 0)] * (a.ndim - 1) + [(0, width - N_GATE)])
    wg_col = pad_to(w, LANES).reshape(D_MODEL, GATE_COLS).astype(BF16)
    wg_row = pad_to(w, SUBLANES).reshape(D_MODEL, GATE_ROWS).T.astype(BF16)
    return wg_col, wg_row, pad_to(b, LANES).reshape(1, GATE_COLS), pad_to(b, SUBLANES).reshape(GATE_ROWS, 1)


def _gate_views(g_col, g_row, bsz):
    n = g_col.shape[0]
    rows = g_row.reshape(HEADS, SUBLANES, n // CHUNK, CHUNK).transpose(0, 2, 1, 3)
    return rows, g_col.reshape(bsz, n // bsz, GATE_COLS)


def kernel(x, c, ctx, c_ctx, w_mod, b_mod, norm1_g, w_in, mlstm_gate_b, qk_conv_w, conv_dw_w,
           conv_ln_g, conv_ln_b, w_conv_out, mlstm_norm_g, w_mlstm_out, w_out, norm2_g,
           w_ff1, w_ff2, final_g):
    bsz, t_lat, _ = x.shape
    t_ctx = ctx.shape[1]
    depth = w_mod.shape[0]
    assert bsz < MOD_ROWS and t_lat % GRID_W == 0

    cc = jnp.zeros((MOD_ROWS, D_MODEL), F32).at[:bsz].set(c).at[bsz].set(c_ctx)
    mod_all = _modulation(cc, w_mod, b_mod)

    g_lo = N_MAIN_BLKS * W_BLK
    g_hi = g_lo + N_GATE * HEADS
    row2 = lambda a: a.reshape(1, -1)

    tm_proj = 1024
    tm_mix, tm_ffn = 256, 512
    lat_row = lambda tm: (lambda i: i // (t_lat // tm))
    ctx_row = lambda i: bsz

    xl = x.reshape(bsz * t_lat, D_MODEL)
    xc = ctx.reshape(bsz * t_ctx, D_MODEL)
    for l in range(depth):
        last = l == depth - 1
        mods = mod_all[l].reshape(MOD_ROWS, 1, N_MOD * D_MODEL)
        w_tail = w_in[l, :, g_hi:].astype(BF16)
        gate_ops = _gate_weights(w_in[l, :, g_lo:g_hi], mlstm_gate_b[l])
        w_conv, w_m, w_o = (w_conv_out[l].astype(BF16), w_mlstm_out[l].astype(BF16),
                            w_out[l].astype(BF16))
        w1, w2 = w_ff1[l].astype(BF16), w_ff2[l].astype(BF16)

        proj = functools.partial(_in_proj, mods=mods, g=row2(norm1_g[l]), w_in=w_in, layer=l,
                                 w_tail=w_tail, wg_col=gate_ops[0], wg_row=gate_ops[1],
                                 gb_col=gate_ops[2], gb_row=gate_ops[3], tm=tm_proj)
        p_lat, gcol_lat, grow_lat = proj(xl, row_fn=lat_row(tm_proj))
        p_ctx, gcol_ctx, grow_ctx = proj(xc, row_fn=ctx_row)
        p_lat3 = p_lat.reshape(bsz, t_lat, P_DIM)
        p_ctx3 = p_ctx.reshape(bsz, t_ctx, P_DIM)
        ql, kl = _qk_conv(p_lat3, qk_conv_w[l])
        qc, kc = _qk_conv(p_ctx3, qk_conv_w[l])
        hm = _mlstm(qc, kc, p_ctx3, _gate_views(gcol_ctx, grow_ctx, bsz),
                    ql, kl, p_lat3, _gate_views(gcol_lat, grow_lat, bsz),
                    row2(mlstm_norm_g[l]), ctx_out=not last)

        mix = functools.partial(_mixer_out, mods=mods, dw_w=conv_dw_w[l], ln_g=row2(conv_ln_g[l]),
                                ln_b=row2(conv_ln_b[l]), w_conv=w_conv, w_m=w_m, w_out=w_o, tm=tm_mix)
        ffn = functools.partial(_ffn, mods=mods, g=row2(norm2_g[l]), w1=w1, w2=w2,
                                final_g=row2(final_g))
        xl = mix(p_lat, hm[0].reshape(bsz * t_lat, MLSTM_DIM), xl, row_fn=lat_row(tm_mix), group=GRID_W)
        xl = ffn(xl, row_fn=lat_row(tm_ffn), tm=tm_ffn, final_norm=last)
        if not last:
            xc = mix(p_ctx, hm[1].reshape(bsz * t_ctx, MLSTM_DIM), xc, row_fn=ctx_row, group=t_ctx)
            xc = ffn(xc, row_fn=ctx_row, tm=tm_ffn, final_norm=False)
    return xl.reshape(bsz, t_lat, D_MODEL)
```

```python
import functools

import jax
import jax.numpy as jnp
from jax import lax
from jax.experimental import pallas as pl
from jax.experimental.pallas import tpu as pltpu

D_MODEL = 1024
GRID_W = 64
CONV_WIDTH = 31
CONV_HALO = 16
CONV_LANE_BLK = 256
SUBLANES = 8
BF16_TILE_ROWS = 16
HEADS = 4
MLSTM_DIM = 2 * D_MODEL
HEAD_DIM = MLSTM_DIM // HEADS
QK_CONV_WIDTH = 5
CHUNK = 256
D_FF = 4 * D_MODEL
N_MOD = 6
EPS = 1e-6
M_INIT = -1e30
N_GATE = 4
LANES = 128
GATE_COLS = HEADS * LANES
GATE_ROWS = HEADS * SUBLANES
W_BLK = 2 * D_MODEL
N_MAIN_BLKS = 5
FF_BLK = D_MODEL
P_DIM = 12 * D_MODEL
COL_A, COL_B, COL_GC, COL_GM = 0, 1, 10, 11
COL_Q, COL_K, COL_V, COL_O = 4, 8, 12, 16
MOD_ROWS = 16
VMEM_LIMIT = 56 * 1024 * 1024

BF16 = jnp.bfloat16
F32 = jnp.float32


def _silu(x):
    return x * jax.nn.sigmoid(x)


def _bdot(a, b):
    return jnp.dot(a, b, preferred_element_type=F32)


def _cparams(*sem, **kwargs):
    return pltpu.CompilerParams(dimension_semantics=sem, vmem_limit_bytes=VMEM_LIMIT, **kwargs)


def _mod_kernel(cc_ref, w_ref, b_ref, o_ref):
    s = _silu(cc_ref[...]).astype(BF16)
    o_ref[...] = _bdot(s, w_ref[...].astype(BF16)) + b_ref[...]


def _modulation(cc, w_mod, b_mod):
    depth = w_mod.shape[0]
    tn = D_MODEL
    return pl.pallas_call(
        _mod_kernel,
        grid=(depth, N_MOD * D_MODEL // tn),
        in_specs=[
            pl.BlockSpec((MOD_ROWS, D_MODEL), lambda l, j: (0, 0)),
            pl.BlockSpec((None, D_MODEL, tn), lambda l, j: (l, 0, j)),
            pl.BlockSpec((None, 1, tn), lambda l, j: (l, 0, j)),
        ],
        out_specs=pl.BlockSpec((None, MOD_ROWS, tn), lambda l, j: (l, 0, j)),
        out_shape=jax.ShapeDtypeStruct((depth, MOD_ROWS, N_MOD * D_MODEL), F32),
        compiler_params=_cparams("arbitrary", "arbitrary"),
        name="modulation",
    )(cc, w_mod, b_mod.reshape(depth, 1, N_MOD * D_MODEL))


def _mod_spec(k, row_fn):
    return pl.BlockSpec((None, 1, D_MODEL), lambda *idx: (row_fn(*idx), 0, k))


def _rms_modulate(x, g, shift, scale):
    y = x * lax.rsqrt(jnp.mean(x * x, axis=-1, keepdims=True) + EPS) * g
    return y * (1.0 + scale) + shift


def _in_proj_kernel(x_ref, sh_ref, sc_ref, g_ref, w_ref, wt_ref, wgc_ref, wgr_ref, gbc_ref, gbr_ref,
                    p_ref, gcol_ref, grow_ref, h_scr):
    j = pl.program_id(1)

    @pl.when(j == 0)
    def _():
        h = _rms_modulate(x_ref[...], g_ref[...], sh_ref[...], sc_ref[...]).astype(BF16)
        h_scr[...] = h
        gcol_ref[...] = _bdot(h, wgc_ref[...]) + gbc_ref[...]
        grow_ref[...] = lax.dot_general(wgr_ref[...], h, (((1,), (1,)), ((), ())),
                                        preferred_element_type=F32) + gbr_ref[...]

    @pl.when(j < N_MAIN_BLKS)
    def _():
        p_ref[...] = _bdot(h_scr[...], w_ref[...]).astype(BF16)

    @pl.when(j == N_MAIN_BLKS)
    def _():
        p_ref[...] = _bdot(h_scr[...], wt_ref[...]).astype(BF16)


def _in_proj(x, mods, g, w_in, layer, w_tail, wg_col, wg_row, gb_col, gb_row, row_fn, tm):
    n = x.shape[0]
    const = lambda shape: pl.BlockSpec(shape, lambda i, j: (0, 0), pipeline_mode=pl.Buffered(1))
    return pl.pallas_call(
        _in_proj_kernel,
        grid=(n // tm, N_MAIN_BLKS + 1),
        in_specs=[
            pl.BlockSpec((tm, D_MODEL), lambda i, j: (i, 0)),
            _mod_spec(0, lambda i, j: row_fn(i)),
            _mod_spec(1, lambda i, j: row_fn(i)),
            const((1, D_MODEL)),
            pl.BlockSpec((None, D_MODEL, W_BLK),
                         lambda i, j: (layer, 0, jnp.minimum(j, N_MAIN_BLKS - 1))),
            const((D_MODEL, W_BLK)),
            const((D_MODEL, GATE_COLS)), const((GATE_ROWS, D_MODEL)),
            const((1, GATE_COLS)), const((GATE_ROWS, 1)),
        ],
        out_specs=[
            pl.BlockSpec((tm, W_BLK), lambda i, j: (i, j)),
            pl.BlockSpec((tm, GATE_COLS), lambda i, j: (i, 0)),
            pl.BlockSpec((GATE_ROWS, tm), lambda i, j: (0, i)),
        ],
        out_shape=[
            jax.ShapeDtypeStruct((n, P_DIM), BF16),
            jax.ShapeDtypeStruct((n, GATE_COLS), F32),
            jax.ShapeDtypeStruct((GATE_ROWS, n), F32),
        ],
        scratch_shapes=[pltpu.VMEM((tm, D_MODEL), BF16)],
        compiler_params=_cparams("arbitrary", "arbitrary"),
        name="in_proj",
    )(x, mods, mods, g, w_in, w_tail, wg_col, wg_row, gb_col, gb_row)


def _qk_conv_kernel(q_ref, k_ref, wq_ref, wk_ref, qo_ref, ko_ref):
    t = q_ref.shape[0]
    half = QK_CONV_WIDTH // 2
    edge = 2 * BF16_TILE_ROWS

    def taps(x, w_ref, valid_fn):
        n = x.shape[0]
        acc = x * w_ref[half:half + 1, :]
        for off in range(-half, half + 1):
            if off == 0:
                continue
            shifted = pltpu.roll(x, (-off) % n, 0)
            if valid_fn is not None:
                shifted = jnp.where(valid_fn(off), shifted, 0.0)
            acc = acc + shifted * w_ref[off + half:off + half + 1, :]
        return acc

    def conv(x_ref, w_ref, o_ref, scale):
        def out(acc):
            y = _silu(acc)
            return (y if scale is None else y * scale).astype(BF16)

        o_ref[...] = out(taps(x_ref[...].astype(F32), w_ref, None))
        row = lax.broadcasted_iota(jnp.int32, (edge, 1), 0)
        top = taps(x_ref[0:edge, :].astype(F32), w_ref, lambda off: row + off >= 0)
        o_ref[0:BF16_TILE_ROWS, :] = out(top[0:BF16_TILE_ROWS, :])
        bot = taps(x_ref[t - edge:t, :].astype(F32), w_ref, lambda off: row + off < edge)
        o_ref[t - BF16_TILE_ROWS:t, :] = out(bot[edge - BF16_TILE_ROWS:edge, :])

    conv(q_ref, wq_ref, qo_ref, None)
    conv(k_ref, wk_ref, ko_ref, HEAD_DIM ** -0.5)


def _qk_conv(p, qk_w):
    bsz, t, _ = p.shape
    out = jax.ShapeDtypeStruct((bsz, t, MLSTM_DIM), BF16)
    blk = lambda c0: pl.BlockSpec((None, t, HEAD_DIM), lambda b, c: (b, 0, c0 + c))
    wblk = lambda c0: pl.BlockSpec((QK_CONV_WIDTH, HEAD_DIM), lambda b, c: (0, c0 + c))
    return pl.pallas_call(
        _qk_conv_kernel,
        grid=(bsz, HEADS),
        in_specs=[blk(COL_Q), blk(COL_K), wblk(0), wblk(HEADS)],
        out_specs=[blk(0), blk(0)],
        out_shape=[out, out],
        compiler_params=_cparams("arbitrary", "arbitrary"),
        name="qk_conv",
    )(p, p, qk_w, qk_w)


def _mlstm_chunk(q, k, v, g_row, g_col, state, reverse):
    c_ref, c16_ref, n_ref, m_ref = state
    gi, gf = (2, 3) if reverse else (0, 1)
    i_row, i_col = g_row[gi:gi + 1, :], g_col[:, gi:gi + 1]
    lf_row = jax.nn.log_sigmoid(g_row[gf:gf + 1, :])
    lf_col = jax.nn.log_sigmoid(g_col[:, gf:gf + 1])
    r = lax.broadcasted_iota(jnp.int32, (CHUNK, CHUNK), 0)
    c = lax.broadcasted_iota(jnp.int32, (CHUNK, CHUNK), 1)
    mask = (c >= r) if reverse else (c <= r)
    mask_t = (r >= c) if reverse else (r <= c)
    b_col = jnp.sum(jnp.where(mask, lf_row, 0.0), axis=1, keepdims=True)
    b_row = jnp.sum(jnp.where(mask_t, lf_col, 0.0), axis=0, keepdims=True)
    b_last = jnp.sum(lf_row, axis=1, keepdims=True)
    m_st = m_ref[...]

    log_d = jnp.where(mask, b_col - b_row + i_row, -jnp.inf)
    inter = b_col + m_st
    m_t = jnp.maximum(inter, jnp.max(log_d, axis=1, keepdims=True))
    w_intra = jnp.exp(log_d - m_t)
    w_inter = jnp.exp(inter - m_t)
    s = lax.dot_general(q, k, (((1,), (1,)), ((), ())), preferred_element_type=F32) * w_intra
    num = w_inter * _bdot(q, c16_ref[...]) + _bdot(s.astype(BF16), v)
    n16 = jnp.broadcast_to(n_ref[...], (SUBLANES, HEAD_DIM)).astype(BF16)
    qn = lax.dot_general(q, n16, (((1,), (1,)), ((), ())), preferred_element_type=F32)[:, 0:1]
    den = w_inter * qn + jnp.sum(s, axis=1, keepdims=True)
    h = num / jnp.maximum(jnp.abs(den), jnp.exp(-m_t))

    g_end = b_last - b_col + i_col
    inter_end = b_last + m_st
    m_new = jnp.maximum(inter_end, jnp.max(g_end, axis=0, keepdims=True))
    ws = jnp.exp(g_end - m_new)
    we = jnp.exp(inter_end - m_new)
    kw = k.astype(F32) * ws
    kv = lax.dot_general(kw.astype(BF16), v, (((0,), (0,)), ((), ())), preferred_element_type=F32)
    c_new = we * c_ref[...] + kv
    c_ref[...] = c_new
    c16_ref[...] = c_new.astype(BF16)
    n_ref[...] = we * n_ref[...] + jnp.sum(kw, axis=0, keepdims=True)
    m_ref[...] = m_new
    return h


def _mlstm_kernel(qc_ref, kc_ref, vc_ref, oc_ref, growc_ref, gcolc_ref,
                  ql_ref, kl_ref, vl_ref, ol_ref, growl_ref, gcoll_ref, ng_ref, *rest, ctx_out):
    if ctx_out:
        hl_ref, hc_ref = rest[:2]
        scratch = rest[2:]
        accc_ref = scratch[9]
    else:
        hl_ref, hc_ref, accc_ref = rest[0], None, None
        scratch = rest[1:]
    state_f, state_b, accl_ref = scratch[0:4], scratch[4:8], scratch[8]
    nc_ctx = qc_ref.shape[0] // CHUNK
    nc_lat = ql_ref.shape[0] // CHUNK
    assert (nc_ctx == 1 or nc_ctx % 2 == 0) and nc_lat % 2 == 0

    for c_ref, c16_ref, n_ref, m_ref in (state_f, state_b):
        c_ref[...] = jnp.zeros_like(c_ref)
        c16_ref[...] = jnp.zeros_like(c16_ref)
        n_ref[...] = jnp.zeros_like(n_ref)
        m_ref[...] = jnp.full_like(m_ref, M_INIT)

    def finish(h, o_ref, out_ref, rows):
        hn = h * lax.rsqrt(jnp.mean(h * h, axis=-1, keepdims=True) + EPS) * ng_ref[...]
        out_ref[rows, :] = (hn * jax.nn.sigmoid(o_ref[rows, :].astype(F32))).astype(BF16)

    def pair_step(i, n_chunks, refs, o_ref, out_ref, acc_ref, closing):
        q_ref, k_ref, v_ref, grow_ref, gcol_ref = refs
        steps = ((i, state_f, False, closing[0]), (n_chunks - 1 - i, state_b, True, closing[1]))
        for j, state, reverse, close in steps:
            if isinstance(j, int):
                rows = pl.ds(j * CHUNK, CHUNK)
            else:
                rows = pl.ds(pl.multiple_of(j * CHUNK, CHUNK), CHUNK)
            h = _mlstm_chunk(q_ref[rows, :], k_ref[rows, :], v_ref[rows, :],
                             grow_ref[j], gcol_ref[rows, :], state, reverse)
            if out_ref is None:
                continue
            if close:
                finish(acc_ref[rows, :] + h, o_ref, out_ref, rows)
            else:
                acc_ref[rows, :] = h

    ctx = ((qc_ref, kc_ref, vc_ref, growc_ref, gcolc_ref), oc_ref, hc_ref, accc_ref)
    lat = ((ql_ref, kl_ref, vl_ref, growl_ref, gcoll_ref), ol_ref, hl_ref, accl_ref)
    for n_chunks, args in ((nc_ctx, ctx), (nc_lat, lat)):
        half = n_chunks // 2

        def opening(i, carry, n_chunks=n_chunks, args=args):
            pair_step(i, n_chunks, *args, closing=(False, False))
            return carry

        def closing(i, carry, n_chunks=n_chunks, args=args):
            pair_step(i, n_chunks, *args, closing=(True, True))
            return carry

        if n_chunks == 1:
            pair_step(0, 1, *args, closing=(False, True))
        elif half == 1:
            opening(0, 0)
            closing(1, 0)
        else:
            lax.fori_loop(0, half, opening, 0)
            lax.fori_loop(half, n_chunks, closing, 0)


def _mlstm(qc, kc, p_ctx, gates_ctx, ql, kl, p_lat, gates_lat, norm_g, ctx_out):
    bsz, t_ctx, _ = qc.shape
    t_lat = ql.shape[1]
    blk = lambda t, c0: pl.BlockSpec((None, t, HEAD_DIM), lambda b, h: (b, 0, c0 + h))

    def stream_specs(t):
        return [
            blk(t, 0), blk(t, 0), blk(t, COL_V), blk(t, COL_O),
            pl.BlockSpec((None, t // CHUNK, SUBLANES, CHUNK), lambda b, h: (h, b, 0, 0)),
            pl.BlockSpec((None, t, LANES), lambda b, h: (b, 0, h)),
        ]

    in_specs = (stream_specs(t_ctx) + stream_specs(t_lat)
                + [pl.BlockSpec((1, HEAD_DIM), lambda b, h: (0, h))])
    out_specs = [blk(t_lat, 0)]
    out_shape = [jax.ShapeDtypeStruct((bsz, t_lat, MLSTM_DIM), BF16)]
    state = [pltpu.VMEM((HEAD_DIM, HEAD_DIM), F32), pltpu.VMEM((HEAD_DIM, HEAD_DIM), BF16),
             pltpu.VMEM((1, HEAD_DIM), F32), pltpu.VMEM((1, 1), F32)]
    scratch = state + state + [pltpu.VMEM((t_lat, HEAD_DIM), F32)]
    if ctx_out:
        out_specs.append(blk(t_ctx, 0))
        out_shape.append(jax.ShapeDtypeStruct((bsz, t_ctx, MLSTM_DIM), BF16))
        scratch.append(pltpu.VMEM((t_ctx, HEAD_DIM), F32))
    return pl.pallas_call(
        functools.partial(_mlstm_kernel, ctx_out=ctx_out),
        grid=(bsz, HEADS),
        in_specs=in_specs,
        out_specs=out_specs,
        out_shape=out_shape,
        scratch_shapes=scratch,
        compiler_params=_cparams("arbitrary", "arbitrary"),
        name="mlstm",
    )(qc, kc, p_ctx, p_ctx, *gates_ctx, ql, kl, p_lat, p_lat, *gates_lat, norm_g)


def _mix_ffn_kernel(a_ref, b_ref, gc_ref, gm_ref, hm_ref, x_ref, ga1_ref, sh2_ref, sc2_ref, ga2_ref,
                    dw_ref, lng_ref, lnb_ref, wc_ref, wm_ref, wo_ref, g2_ref, w1_ref, w2_ref, fg_ref,
                    o_ref, pad_ref, rot_ref, cv_ref, *, group, final_norm):
    tm = a_ref.shape[0]
    stride = group + 2 * CONV_HALO
    rows = pad_ref.shape[0]
    lane_blk = rot_ref.shape[2]
    u = a_ref[...].astype(F32) * jax.nn.sigmoid(b_ref[...].astype(F32))
    halo = jnp.zeros((CONV_HALO, D_MODEL), F32)
    for g in range(tm // group):
        base = g * stride
        pad_ref[base:base + CONV_HALO, :] = halo
        pad_ref[base + CONV_HALO:base + CONV_HALO + group, :] = u[g * group:(g + 1) * group, :]
        pad_ref[base + CONV_HALO + group:base + stride, :] = halo
    first = CONV_HALO - CONV_WIDTH // 2
    for cb in range(D_MODEL // lane_blk):
        cols = slice(cb * lane_blk, (cb + 1) * lane_blk)
        for d in range(1, SUBLANES):
            rot_ref[d - 1, 0:rows - SUBLANES, :] = pad_ref[d:d + rows - SUBLANES, cols]
        for g in range(tm // group):
            acc = jnp.zeros((group, lane_blk), F32)
            for j in range(CONV_WIDTH):
                tile, d = divmod(g * stride + first + j, SUBLANES)
                lo = tile * SUBLANES
                slab = pad_ref[lo:lo + group, cols] if d == 0 else rot_ref[d - 1, lo:lo + group, :]
                acc = acc + slab * dw_ref[j:j + 1, cols]
            cv_ref[g * group:(g + 1) * group, cols] = acc
    cv = cv_ref[...]
    mu = jnp.mean(cv, axis=-1, keepdims=True)
    var = jnp.mean(jnp.square(cv - mu), axis=-1, keepdims=True)
    ln = (cv - mu) * lax.rsqrt(var + EPS) * lng_ref[...] + lnb_ref[...]
    y_conv = _bdot(_silu(ln).astype(BF16), wc_ref[...])
    y_m = _bdot(hm_ref[...], wm_ref[...])
    y = (jax.nn.sigmoid(gc_ref[...].astype(F32)) * y_conv
         + jax.nn.sigmoid(gm_ref[...].astype(F32)) * y_m)
    x1 = x_ref[...] + ga1_ref[...] * _bdot(y.astype(BF16), wo_ref[...])

    h2 = _rms_modulate(x1, g2_ref[...], sh2_ref[...], sc2_ref[...]).astype(BF16)
    acc = jnp.zeros(x1.shape, F32)
    for cb in range(D_FF // FF_BLK):
        cols = slice(cb * FF_BLK, (cb + 1) * FF_BLK)
        t = jnp.square(jnp.maximum(_bdot(h2, w1_ref[:, cols]), 0.0))
        acc = acc + _bdot(t.astype(BF16), w2_ref[cols, :])
    x2 = x1 + ga2_ref[...] * acc
    if final_norm:
        x2 = x2 * lax.rsqrt(jnp.mean(x2 * x2, axis=-1, keepdims=True) + EPS) * fg_ref[...]
    o_ref[...] = x2


def _mix_ffn(p, hm, x, mods, dw_w, ln_g, ln_b, w_conv, w_m, w_out, g2, w1, w2, final_g,
             row_fn, tm, group, final_norm):
    n = x.shape[0]
    pad_rows = tm // group * (group + 2 * CONV_HALO)
    const = lambda shape: pl.BlockSpec(shape, lambda i: (0, 0), pipeline_mode=pl.Buffered(1))
    pblk = lambda c: pl.BlockSpec((tm, D_MODEL), lambda i: (i, c))
    mod = lambda k: _mod_spec(k, row_fn)
    return pl.pallas_call(
        functools.partial(_mix_ffn_kernel, group=group, final_norm=final_norm),
        grid=(n // tm,),
        in_specs=[
            pblk(COL_A), pblk(COL_B), pblk(COL_GC), pblk(COL_GM),
            pl.BlockSpec((tm, MLSTM_DIM), lambda i: (i, 0)),
            pl.BlockSpec((tm, D_MODEL), lambda i: (i, 0)),
            mod(2), mod(3), mod(4), mod(5),
            const((CONV_WIDTH, D_MODEL)), const((1, D_MODEL)), const((1, D_MODEL)),
            const((D_MODEL, D_MODEL)), const((MLSTM_DIM, D_MODEL)), const((D_MODEL, D_MODEL)),
            const((1, D_MODEL)), const((D_MODEL, D_FF)), const((D_FF, D_MODEL)), const((1, D_MODEL)),
        ],
        out_specs=pl.BlockSpec((tm, D_MODEL), lambda i: (i, 0)),
        out_shape=jax.ShapeDtypeStruct((n, D_MODEL), F32),
        scratch_shapes=[
            pltpu.VMEM((pad_rows, D_MODEL), F32),
            pltpu.VMEM((SUBLANES - 1, pad_rows, CONV_LANE_BLK), F32),
            pltpu.VMEM((tm, D_MODEL), F32),
        ],
        compiler_params=_cparams("arbitrary"),
        name="mix_ffn",
    )(p, p, p, p, hm, x, mods, mods, mods, mods, dw_w, ln_g, ln_b, w_conv, w_m, w_out, g2, w1, w2,
      final_g)


def _gate_weights(w_g, b_g):
    w = w_g.reshape(D_MODEL, N_GATE, HEADS).transpose(0, 2, 1)
    b = b_g.reshape(N_GATE, HEADS).T
    pad_to = lambda a, width: jnp.pad(a, [(0, 0)] * (a.ndim - 1) + [(0, width - N_GATE)])
    wg_col = pad_to(w, LANES).reshape(D_MODEL, GATE_COLS).astype(BF16)
    wg_row = pad_to(w, SUBLANES).reshape(D_MODEL, GATE_ROWS).T.astype(BF16)
    return wg_col, wg_row, pad_to(b, LANES).reshape(1, GATE_COLS), pad_to(b, SUBLANES).reshape(GATE_ROWS, 1)


def _gate_views(g_col, g_row, bsz):
    n = g_col.shape[0]
    rows = g_row.reshape(HEADS, SUBLANES, n // CHUNK, CHUNK).transpose(0, 2, 1, 3)
    return rows, g_col.reshape(bsz, n // bsz, GATE_COLS)


def kernel(x, c, ctx, c_ctx, w_mod, b_mod, norm1_g, w_in, mlstm_gate_b, qk_conv_w, conv_dw_w,
           conv_ln_g, conv_ln_b, w_conv_out, mlstm_norm_g, w_mlstm_out, w_out, norm2_g,
           w_ff1, w_ff2, final_g):
    bsz, t_lat, _ = x.shape
    t_ctx = ctx.shape[1]
    depth = w_mod.shape[0]
    assert bsz < MOD_ROWS and t_lat % GRID_W == 0

    cc = jnp.zeros((MOD_ROWS, D_MODEL), F32).at[:bsz].set(c).at[bsz].set(c_ctx)
    mod_all = _modulation(cc, w_mod, b_mod)

    g_lo = N_MAIN_BLKS * W_BLK
    g_hi = g_lo + N_GATE * HEADS
    row2 = lambda a: a.reshape(1, -1)

    tm_proj = 1024
    tm_mix = 256
    lat_row = lambda tm: (lambda i: i // (t_lat // tm))
    ctx_row = lambda i: bsz

    w_in16 = w_in.astype(BF16)
    xl = x.reshape(bsz * t_lat, D_MODEL)
    xc = ctx.reshape(bsz * t_ctx, D_MODEL)
    for l in range(depth):
        last = l == depth - 1
        mods = mod_all[l].reshape(MOD_ROWS, 1, N_MOD * D_MODEL)
        w_tail = w_in16[l, :, g_hi:]
        gate_ops = _gate_weights(w_in[l, :, g_lo:g_hi], mlstm_gate_b[l])
        w_conv, w_m, w_o = (w_conv_out[l].astype(BF16), w_mlstm_out[l].astype(BF16),
                            w_out[l].astype(BF16))
        w1, w2 = w_ff1[l].astype(BF16), w_ff2[l].astype(BF16)

        proj = functools.partial(_in_proj, mods=mods, g=row2(norm1_g[l]), w_in=w_in16, layer=l,
                                 w_tail=w_tail, wg_col=gate_ops[0], wg_row=gate_ops[1],
                                 gb_col=gate_ops[2], gb_row=gate_ops[3], tm=tm_proj)
        p_lat, gcol_lat, grow_lat = proj(xl, row_fn=lat_row(tm_proj))
        p_ctx, gcol_ctx, grow_ctx = proj(xc, row_fn=ctx_row)
        p_lat3 = p_lat.reshape(bsz, t_lat, P_DIM)
        p_ctx3 = p_ctx.reshape(bsz, t_ctx, P_DIM)
        ql, kl = _qk_conv(p_lat3, qk_conv_w[l])
        qc, kc = _qk_conv(p_ctx3, qk_conv_w[l])
        hm = _mlstm(qc, kc, p_ctx3, _gate_views(gcol_ctx, grow_ctx, bsz),
                    ql, kl, p_lat3, _gate_views(gcol_lat, grow_lat, bsz),
                    row2(mlstm_norm_g[l]), ctx_out=not last)

        mix_ffn = functools.partial(
            _mix_ffn, mods=mods, dw_w=conv_dw_w[l], ln_g=row2(conv_ln_g[l]), ln_b=row2(conv_ln_b[l]),
            w_conv=w_conv, w_m=w_m, w_out=w_o, g2=row2(norm2_g[l]), w1=w1, w2=w2,
            final_g=row2(final_g), tm=tm_mix)
        xl = mix_ffn(p_lat, hm[0].reshape(bsz * t_lat, MLSTM_DIM), xl, row_fn=lat_row(tm_mix),
                     group=GRID_W, final_norm=last)
        if not last:
            xc = mix_ffn(p_ctx, hm[1].reshape(bsz * t_ctx, MLSTM_DIM), xc, row_fn=ctx_row,
                         group=t_ctx, final_norm=False)
    return xl.reshape(bsz, t_lat, D_MODEL)
```

```python
import functools

import jax
import jax.numpy as jnp
from jax import lax
from jax.experimental import pallas as pl
from jax.experimental.pallas import tpu as pltpu

D_MODEL = 1024
GRID_W = 64
CONV_WIDTH = 31
CONV_HALO = 16
CONV_LANE_BLK = 256
SUBLANES = 8
BF16_TILE_ROWS = 16
HEADS = 4
MLSTM_DIM = 2 * D_MODEL
HEAD_DIM = MLSTM_DIM // HEADS
QK_CONV_WIDTH = 5
CHUNK = 256
D_FF = 4 * D_MODEL
N_MOD = 6
EPS = 1e-6
M_INIT = -1e30
N_GATE = 4
LANES = 128
GATE_COLS = HEADS * LANES
GATE_ROWS = HEADS * SUBLANES
W_BLK = 2 * D_MODEL
N_MAIN_BLKS = 5
FF_BLK = D_MODEL
P_DIM = 8 * D_MODEL
COL_A, COL_B, COL_GC, COL_GM = 0, 1, 6, 7
COL_Q, COL_K = 4, 8
BLK_V, BLK_O = 3, 4
MOD_ROWS = 16
VMEM_LIMIT = 56 * 1024 * 1024

BF16 = jnp.bfloat16
F32 = jnp.float32


def _silu(x):
    return x * jax.nn.sigmoid(x)


def _bdot(a, b):
    return jnp.dot(a, b, preferred_element_type=F32)


def _cparams(*sem, **kwargs):
    return pltpu.CompilerParams(dimension_semantics=sem, vmem_limit_bytes=VMEM_LIMIT, **kwargs)


def _mod_kernel(cc_ref, w_ref, b_ref, o_ref):
    s = _silu(cc_ref[...]).astype(BF16)
    o_ref[...] = _bdot(s, w_ref[...].astype(BF16)) + b_ref[...]


def _modulation(cc, w_mod, b_mod):
    depth = w_mod.shape[0]
    tn = D_MODEL
    return pl.pallas_call(
        _mod_kernel,
        grid=(depth, N_MOD * D_MODEL // tn),
        in_specs=[
            pl.BlockSpec((MOD_ROWS, D_MODEL), lambda l, j: (0, 0)),
            pl.BlockSpec((None, D_MODEL, tn), lambda l, j: (l, 0, j)),
            pl.BlockSpec((None, 1, tn), lambda l, j: (l, 0, j)),
        ],
        out_specs=pl.BlockSpec((None, MOD_ROWS, tn), lambda l, j: (l, 0, j)),
        out_shape=jax.ShapeDtypeStruct((depth, MOD_ROWS, N_MOD * D_MODEL), F32),
        compiler_params=_cparams("arbitrary", "arbitrary"),
        name="modulation",
    )(cc, w_mod, b_mod.reshape(depth, 1, N_MOD * D_MODEL))


def _mod_spec(k, row_fn):
    return pl.BlockSpec((None, 1, D_MODEL), lambda *idx: (row_fn(*idx), 0, k))


def _rms_modulate(x, g, shift, scale):
    y = x * lax.rsqrt(jnp.mean(x * x, axis=-1, keepdims=True) + EPS) * g
    return y * (1.0 + scale) + shift


def _in_proj_kernel(x_ref, sh_ref, sc_ref, g_ref, w_ref, wt_ref, wgc_ref, wgr_ref, gbc_ref, gbr_ref,
                    p_ref, v_ref, o_ref, gcol_ref, grow_ref, h_scr):
    j = pl.program_id(1)

    def split_heads(ref, val, width):
        for hd in range(HEADS):
            ref[hd] = val[:, hd * width:(hd + 1) * width].astype(ref.dtype)

    @pl.when(j == 0)
    def _():
        h = _rms_modulate(x_ref[...], g_ref[...], sh_ref[...], sc_ref[...]).astype(BF16)
        h_scr[...] = h
        split_heads(gcol_ref, _bdot(h, wgc_ref[...]) + gbc_ref[...], LANES)
        grow_ref[...] = lax.dot_general(wgr_ref[...], h, (((1,), (1,)), ((), ())),
                                        preferred_element_type=F32) + gbr_ref[...]

    @pl.when(j < BLK_V)
    def _():
        p_ref[...] = _bdot(h_scr[...], w_ref[...]).astype(BF16)

    @pl.when(j == BLK_V)
    def _():
        split_heads(v_ref, _bdot(h_scr[...], w_ref[...]), HEAD_DIM)

    @pl.when(j == BLK_O)
    def _():
        split_heads(o_ref, _bdot(h_scr[...], w_ref[...]), HEAD_DIM)

    @pl.when(j == N_MAIN_BLKS)
    def _():
        p_ref[...] = _bdot(h_scr[...], wt_ref[...]).astype(BF16)


def _in_proj(x, mods, g, w_in, layer, w_tail, wg_col, wg_row, gb_col, gb_row, row_fn, tm, seq):
    n = x.shape[0]
    bsz, per_seq = n // seq, seq // tm
    const = lambda shape: pl.BlockSpec(shape, lambda i, j: (0, 0), pipeline_mode=pl.Buffered(1))
    heads = lambda width: pl.BlockSpec((None, HEADS, tm, width),
                                       lambda i, j: (i // per_seq, 0, i % per_seq, 0))
    p_col = lambda j: jnp.where(j == N_MAIN_BLKS, BLK_V, jnp.minimum(j, BLK_V - 1))
    return pl.pallas_call(
        _in_proj_kernel,
        grid=(n // tm, N_MAIN_BLKS + 1),
        in_specs=[
            pl.BlockSpec((tm, D_MODEL), lambda i, j: (i, 0)),
            _mod_spec(0, lambda i, j: row_fn(i)),
            _mod_spec(1, lambda i, j: row_fn(i)),
            const((1, D_MODEL)),
            pl.BlockSpec((None, D_MODEL, W_BLK),
                         lambda i, j: (layer, 0, jnp.minimum(j, N_MAIN_BLKS - 1))),
            const((D_MODEL, W_BLK)),
            const((D_MODEL, GATE_COLS)), const((GATE_ROWS, D_MODEL)),
            const((1, GATE_COLS)), const((GATE_ROWS, 1)),
        ],
        out_specs=[
            pl.BlockSpec((tm, W_BLK), lambda i, j: (i, p_col(j))),
            heads(HEAD_DIM), heads(HEAD_DIM), heads(LANES),
            pl.BlockSpec((GATE_ROWS, tm), lambda i, j: (0, i)),
        ],
        out_shape=[
            jax.ShapeDtypeStruct((n, P_DIM), BF16),
            jax.ShapeDtypeStruct((bsz, HEADS, seq, HEAD_DIM), BF16),
            jax.ShapeDtypeStruct((bsz, HEADS, seq, HEAD_DIM), BF16),
            jax.ShapeDtypeStruct((bsz, HEADS, seq, LANES), F32),
            jax.ShapeDtypeStruct((GATE_ROWS, n), F32),
        ],
        scratch_shapes=[pltpu.VMEM((tm, D_MODEL), BF16)],
        compiler_params=_cparams("arbitrary", "arbitrary"),
        name="in_proj",
    )(x, mods, mods, g, w_in, w_tail, wg_col, wg_row, gb_col, gb_row)


def _qk_conv_kernel(q_ref, k_ref, wq_ref, wk_ref, qo_ref, ko_ref):
    t = q_ref.shape[0]
    half = QK_CONV_WIDTH // 2
    edge = 2 * BF16_TILE_ROWS

    def taps(x, w_ref, valid_fn):
        n = x.shape[0]
        acc = x * w_ref[half:half + 1, :]
        for off in range(-half, half + 1):
            if off == 0:
                continue
            shifted = pltpu.roll(x, (-off) % n, 0)
            if valid_fn is not None:
                shifted = jnp.where(valid_fn(off), shifted, 0.0)
            acc = acc + shifted * w_ref[off + half:off + half + 1, :]
        return acc

    def conv(x_ref, w_ref, o_ref, scale):
        def out(acc):
            y = _silu(acc)
            return (y if scale is None else y * scale).astype(BF16)

        o_ref[...] = out(taps(x_ref[...].astype(F32), w_ref, None))
        row = lax.broadcasted_iota(jnp.int32, (edge, 1), 0)
        top = taps(x_ref[0:edge, :].astype(F32), w_ref, lambda off: row + off >= 0)
        o_ref[0:BF16_TILE_ROWS, :] = out(top[0:BF16_TILE_ROWS, :])
        bot = taps(x_ref[t - edge:t, :].astype(F32), w_ref, lambda off: row + off < edge)
        o_ref[t - BF16_TILE_ROWS:t, :] = out(bot[edge - BF16_TILE_ROWS:edge, :])

    conv(q_ref, wq_ref, qo_ref, None)
    conv(k_ref, wk_ref, ko_ref, HEAD_DIM ** -0.5)


def _qk_conv(p, qk_w):
    bsz, t, _ = p.shape
    out = jax.ShapeDtypeStruct((bsz, HEADS, t, HEAD_DIM), BF16)
    blk = lambda c0: pl.BlockSpec((None, t, HEAD_DIM), lambda b, c: (b, 0, c0 + c))
    wblk = lambda c0: pl.BlockSpec((QK_CONV_WIDTH, HEAD_DIM), lambda b, c: (0, c0 + c))
    head_blk = pl.BlockSpec((None, None, t, HEAD_DIM), lambda b, c: (b, c, 0, 0))
    return pl.pallas_call(
        _qk_conv_kernel,
        grid=(bsz, HEADS),
        in_specs=[blk(COL_Q), blk(COL_K), wblk(0), wblk(HEADS)],
        out_specs=[head_blk, head_blk],
        out_shape=[out, out],
        compiler_params=_cparams("arbitrary", "arbitrary"),
        name="qk_conv",
    )(p, p, qk_w, qk_w)


def _mlstm_chunk(q, k, v, g_row, g_col, state, reverse):
    c_ref, c16_ref, n_ref, m_ref = state
    gi, gf = (2, 3) if reverse else (0, 1)
    i_row, i_col = g_row[gi:gi + 1, :], g_col[:, gi:gi + 1]
    lf_row = jax.nn.log_sigmoid(g_row[gf:gf + 1, :])
    lf_col = jax.nn.log_sigmoid(g_col[:, gf:gf + 1])
    r = lax.broadcasted_iota(jnp.int32, (CHUNK, CHUNK), 0)
    c = lax.broadcasted_iota(jnp.int32, (CHUNK, CHUNK), 1)
    mask = (c >= r) if reverse else (c <= r)
    mask_t = (r >= c) if reverse else (r <= c)
    b_col = jnp.sum(jnp.where(mask, lf_row, 0.0), axis=1, keepdims=True)
    b_row = jnp.sum(jnp.where(mask_t, lf_col, 0.0), axis=0, keepdims=True)
    b_last = jnp.sum(lf_row, axis=1, keepdims=True)
    m_st = m_ref[...]

    log_d = jnp.where(mask, b_col - b_row + i_row, -jnp.inf)
    inter = b_col + m_st
    m_t = jnp.maximum(inter, jnp.max(log_d, axis=1, keepdims=True))
    w_intra = jnp.exp(log_d - m_t)
    w_inter = jnp.exp(inter - m_t)
    s = lax.dot_general(q, k, (((1,), (1,)), ((), ())), preferred_element_type=F32) * w_intra
    num = w_inter * _bdot(q, c16_ref[...]) + _bdot(s.astype(BF16), v)
    n16 = jnp.broadcast_to(n_ref[...], (SUBLANES, HEAD_DIM)).astype(BF16)
    qn = lax.dot_general(q, n16, (((1,), (1,)), ((), ())), preferred_element_type=F32)[:, 0:1]
    den = w_inter * qn + jnp.sum(s, axis=1, keepdims=True)
    h = num / jnp.maximum(jnp.abs(den), jnp.exp(-m_t))

    g_end = b_last - b_col + i_col
    inter_end = b_last + m_st
    m_new = jnp.maximum(inter_end, jnp.max(g_end, axis=0, keepdims=True))
    ws = jnp.exp(g_end - m_new)
    we = jnp.exp(inter_end - m_new)
    kw = k.astype(F32) * ws
    kv = lax.dot_general(kw.astype(BF16), v, (((0,), (0,)), ((), ())), preferred_element_type=F32)
    c_new = we * c_ref[...] + kv
    c_ref[...] = c_new
    c16_ref[...] = c_new.astype(BF16)
    n_ref[...] = we * n_ref[...] + jnp.sum(kw, axis=0, keepdims=True)
    m_ref[...] = m_new
    return h


def _mlstm_kernel(qc_ref, kc_ref, vc_ref, oc_ref, growc_ref, gcolc_ref,
                  ql_ref, kl_ref, vl_ref, ol_ref, growl_ref, gcoll_ref, ng_ref, *rest, ctx_out):
    if ctx_out:
        hl_ref, hc_ref = rest[:2]
        scratch = rest[2:]
        accc_ref = scratch[9]
    else:
        hl_ref, hc_ref, accc_ref = rest[0], None, None
        scratch = rest[1:]
    state_f, state_b, accl_ref = scratch[0:4], scratch[4:8], scratch[8]
    nc_ctx = qc_ref.shape[0] // CHUNK
    nc_lat = ql_ref.shape[0] // CHUNK
    assert (nc_ctx == 1 or nc_ctx % 2 == 0) and nc_lat % 2 == 0

    for c_ref, c16_ref, n_ref, m_ref in (state_f, state_b):
        c_ref[...] = jnp.zeros_like(c_ref)
        c16_ref[...] = jnp.zeros_like(c16_ref)
        n_ref[...] = jnp.zeros_like(n_ref)
        m_ref[...] = jnp.full_like(m_ref, M_INIT)

    def finish(h, o_ref, out_ref, rows):
        hn = h * lax.rsqrt(jnp.mean(h * h, axis=-1, keepdims=True) + EPS) * ng_ref[...]
        out_ref[rows, :] = (hn * jax.nn.sigmoid(o_ref[rows, :].astype(F32))).astype(BF16)

    def pair_step(i, n_chunks, refs, o_ref, out_ref, acc_ref, closing):
        q_ref, k_ref, v_ref, grow_ref, gcol_ref = refs
        steps = ((i, state_f, False, closing[0]), (n_chunks - 1 - i, state_b, True, closing[1]))
        for j, state, reverse, close in steps:
            if isinstance(j, int):
                rows = pl.ds(j * CHUNK, CHUNK)
            else:
                rows = pl.ds(pl.multiple_of(j * CHUNK, CHUNK), CHUNK)
            h = _mlstm_chunk(q_ref[rows, :], k_ref[rows, :], v_ref[rows, :],
                             grow_ref[j], gcol_ref[rows, :], state, reverse)
            if out_ref is None:
                continue
            if close:
                finish(acc_ref[rows, :] + h, o_ref, out_ref, rows)
            else:
                acc_ref[rows, :] = h

    ctx = ((qc_ref, kc_ref, vc_ref, growc_ref, gcolc_ref), oc_ref, hc_ref, accc_ref)
    lat = ((ql_ref, kl_ref, vl_ref, growl_ref, gcoll_ref), ol_ref, hl_ref, accl_ref)
    for n_chunks, args in ((nc_ctx, ctx), (nc_lat, lat)):
        half = n_chunks // 2

        def opening(i, carry, n_chunks=n_chunks, args=args):
            pair_step(i, n_chunks, *args, closing=(False, False))
            return carry

        def closing(i, carry, n_chunks=n_chunks, args=args):
            pair_step(i, n_chunks, *args, closing=(True, True))
            return carry

        if n_chunks == 1:
            pair_step(0, 1, *args, closing=(False, True))
        elif half == 1:
            opening(0, 0)
            closing(1, 0)
        else:
            lax.fori_loop(0, half, opening, 0)
            lax.fori_loop(half, n_chunks, closing, 0)


def _mlstm(ctx_stream, lat_stream, norm_g, ctx_out):
    bsz, _, t_ctx, _ = ctx_stream[0].shape
    t_lat = lat_stream[0].shape[2]
    head = lambda t, width: pl.BlockSpec((None, None, t, width), lambda b, h: (b, h, 0, 0))

    def stream_specs(t):
        return [head(t, HEAD_DIM)] * 4 + [
            pl.BlockSpec((None, t // CHUNK, SUBLANES, CHUNK), lambda b, h: (h, b, 0, 0)),
            head(t, LANES),
        ]

    in_specs = (stream_specs(t_ctx) + stream_specs(t_lat)
                + [pl.BlockSpec((1, HEAD_DIM), lambda b, h: (0, h))])
    out_specs = [head(t_lat, HEAD_DIM)]
    out_shape = [jax.ShapeDtypeStruct((bsz, HEADS, t_lat, HEAD_DIM), BF16)]
    state = [pltpu.VMEM((HEAD_DIM, HEAD_DIM), F32), pltpu.VMEM((HEAD_DIM, HEAD_DIM), BF16),
             pltpu.VMEM((1, HEAD_DIM), F32), pltpu.VMEM((1, 1), F32)]
    scratch = state + state + [pltpu.VMEM((t_lat, HEAD_DIM), F32)]
    if ctx_out:
        out_specs.append(head(t_ctx, HEAD_DIM))
        out_shape.append(jax.ShapeDtypeStruct((bsz, HEADS, t_ctx, HEAD_DIM), BF16))
        scratch.append(pltpu.VMEM((t_ctx, HEAD_DIM), F32))
    return pl.pallas_call(
        functools.partial(_mlstm_kernel, ctx_out=ctx_out),
        grid=(bsz, HEADS),
        in_specs=in_specs,
        out_specs=out_specs,
        out_shape=out_shape,
        scratch_shapes=scratch,
        compiler_params=_cparams("arbitrary", "arbitrary"),
        name="mlstm",
    )(*ctx_stream, *lat_stream, norm_g)


def _mix_ffn_kernel(a_ref, b_ref, gc_ref, gm_ref, hm_ref, x_ref, ga1_ref, sh2_ref, sc2_ref, ga2_ref,
                    dw_ref, lng_ref, lnb_ref, wc_ref, wm_ref, wo_ref, g2_ref, w1_ref, w2_ref, fg_ref,
                    o_ref, pad_ref, rot_ref, cv_ref, *, group, final_norm):
    tm = a_ref.shape[0]
    stride = group + 2 * CONV_HALO
    rows = pad_ref.shape[0]
    lane_blk = rot_ref.shape[2]
    u = a_ref[...].astype(F32) * jax.nn.sigmoid(b_ref[...].astype(F32))
    halo = jnp.zeros((CONV_HALO, D_MODEL), F32)
    for g in range(tm // group):
        base = g * stride
        pad_ref[base:base + CONV_HALO, :] = halo
        pad_ref[base + CONV_HALO:base + CONV_HALO + group, :] = u[g * group:(g + 1) * group, :]
        pad_ref[base + CONV_HALO + group:base + stride, :] = halo
    first = CONV_HALO - CONV_WIDTH // 2
    for cb in range(D_MODEL // lane_blk):
        cols = slice(cb * lane_blk, (cb + 1) * lane_blk)
        for d in range(1, SUBLANES):
            rot_ref[d - 1, 0:rows - SUBLANES, :] = pad_ref[d:d + rows - SUBLANES, cols]
        for g in range(tm // group):
            acc = jnp.zeros((group, lane_blk), F32)
            for j in range(CONV_WIDTH):
                tile, d = divmod(g * stride + first + j, SUBLANES)
                lo = tile * SUBLANES
                slab = pad_ref[lo:lo + group, cols] if d == 0 else rot_ref[d - 1, lo:lo + group, :]
                acc = acc + slab * dw_ref[j:j + 1, cols]
            cv_ref[g * group:(g + 1) * group, cols] = acc
    cv = cv_ref[...]
    mu = jnp.mean(cv, axis=-1, keepdims=True)
    var = jnp.mean(jnp.square(cv - mu), axis=-1, keepdims=True)
    ln = (cv - mu) * lax.rsqrt(var + EPS) * lng_ref[...] + lnb_ref[...]
    y_conv = _bdot(_silu(ln).astype(BF16), wc_ref[...])
    y_m = sum(_bdot(hm_ref[hd], wm_ref[hd * HEAD_DIM:(hd + 1) * HEAD_DIM, :]) for hd in range(HEADS))
    y = (jax.nn.sigmoid(gc_ref[...].astype(F32)) * y_conv
         + jax.nn.sigmoid(gm_ref[...].astype(F32)) * y_m)
    x1 = x_ref[...] + ga1_ref[...] * _bdot(y.astype(BF16), wo_ref[...])

    h2 = _rms_modulate(x1, g2_ref[...], sh2_ref[...], sc2_ref[...]).astype(BF16)
    acc = jnp.zeros(x1.shape, F32)
    for cb in range(D_FF // FF_BLK):
        cols = slice(cb * FF_BLK, (cb + 1) * FF_BLK)
        t = jnp.square(jnp.maximum(_bdot(h2, w1_ref[:, cols]), 0.0))
        acc = acc + _bdot(t.astype(BF16), w2_ref[cols, :])
    x2 = x1 + ga2_ref[...] * acc
    if final_norm:
        x2 = x2 * lax.rsqrt(jnp.mean(x2 * x2, axis=-1, keepdims=True) + EPS) * fg_ref[...]
    o_ref[...] = x2


def _mix_ffn(p, hm, x, mods, dw_w, ln_g, ln_b, w_conv, w_m, w_out, g2, w1, w2, final_g,
             row_fn, tm, group, final_norm):
    n = x.shape[0]
    per_seq = hm.shape[2] // tm
    pad_rows = tm // group * (group + 2 * CONV_HALO)
    const = lambda shape: pl.BlockSpec(shape, lambda i: (0, 0), pipeline_mode=pl.Buffered(1))
    pblk = lambda c: pl.BlockSpec((tm, D_MODEL), lambda i: (i, c))
    mod = lambda k: _mod_spec(k, row_fn)
    return pl.pallas_call(
        functools.partial(_mix_ffn_kernel, group=group, final_norm=final_norm),
        grid=(n // tm,),
        in_specs=[
            pblk(COL_A), pblk(COL_B), pblk(COL_GC), pblk(COL_GM),
            pl.BlockSpec((None, HEADS, tm, HEAD_DIM), lambda i: (i // per_seq, 0, i % per_seq, 0)),
            pl.BlockSpec((tm, D_MODEL), lambda i: (i, 0)),
            mod(2), mod(3), mod(4), mod(5),
            const((CONV_WIDTH, D_MODEL)), const((1, D_MODEL)), const((1, D_MODEL)),
            const((D_MODEL, D_MODEL)), const((MLSTM_DIM, D_MODEL)), const((D_MODEL, D_MODEL)),
            const((1, D_MODEL)), const((D_MODEL, D_FF)), const((D_FF, D_MODEL)), const((1, D_MODEL)),
        ],
        out_specs=pl.BlockSpec((tm, D_MODEL), lambda i: (i, 0)),
        out_shape=jax.ShapeDtypeStruct((n, D_MODEL), F32),
        scratch_shapes=[
            pltpu.VMEM((pad_rows, D_MODEL), F32),
            pltpu.VMEM((SUBLANES - 1, pad_rows, CONV_LANE_BLK), F32),
            pltpu.VMEM((tm, D_MODEL), F32),
        ],
        compiler_params=_cparams("arbitrary"),
        name="mix_ffn",
    )(p, p, p, p, hm, x, mods, mods, mods, mods, dw_w, ln_g, ln_b, w_conv, w_m, w_out, g2, w1, w2,
      final_g)


def _gate_weights(w_g, b_g):
    w = w_g.reshape(D_MODEL, N_GATE, HEADS).transpose(0, 2, 1)
    b = b_g.reshape(N_GATE, HEADS).T
    pad_to = lambda a, width: jnp.pad(a, [(0, 0)] * (a.ndim - 1) + [(0, width - N_GATE)])
    wg_col = pad_to(w, LANES).reshape(D_MODEL, GATE_COLS).astype(BF16)
    wg_row = pad_to(w, SUBLANES).reshape(D_MODEL, GATE_ROWS).T.astype(BF16)
    return wg_col, wg_row, pad_to(b, LANES).reshape(1, GATE_COLS), pad_to(b, SUBLANES).reshape(GATE_ROWS, 1)


def kernel(x, c, ctx, c_ctx, w_mod, b_mod, norm1_g, w_in, mlstm_gate_b, qk_conv_w, conv_dw_w,
           conv_ln_g, conv_ln_b, w_conv_out, mlstm_norm_g, w_mlstm_out, w_out, norm2_g,
           w_ff1, w_ff2, final_g):
    bsz, t_lat, _ = x.shape
    t_ctx = ctx.shape[1]
    depth = w_mod.shape[0]
    assert bsz < MOD_ROWS and t_lat % GRID_W == 0

    cc = jnp.zeros((MOD_ROWS, D_MODEL), F32).at[:bsz].set(c).at[bsz].set(c_ctx)
    mod_all = _modulation(cc, w_mod, b_mod)

    g_lo = N_MAIN_BLKS * W_BLK
    g_hi = g_lo + N_GATE * HEADS
    row2 = lambda a: a.reshape(1, -1)

    tm_proj = 1024
    tm_mix = 256
    lat_row = lambda tm: (lambda i: i // (t_lat // tm))
    ctx_row = lambda i: bsz

    w_in16 = w_in.astype(BF16)
    xl = x.reshape(bsz * t_lat, D_MODEL)
    xc = ctx.reshape(bsz * t_ctx, D_MODEL)
    for l in range(depth):
        last = l == depth - 1
        mods = mod_all[l].reshape(MOD_ROWS, 1, N_MOD * D_MODEL)
        w_tail = w_in16[l, :, g_hi:]
        gate_ops = _gate_weights(w_in[l, :, g_lo:g_hi], mlstm_gate_b[l])
        w_conv, w_m, w_o = (w_conv_out[l].astype(BF16), w_mlstm_out[l].astype(BF16),
                            w_out[l].astype(BF16))
        w1, w2 = w_ff1[l].astype(BF16), w_ff2[l].astype(BF16)

        proj = functools.partial(_in_proj, mods=mods, g=row2(norm1_g[l]), w_in=w_in16, layer=l,
                                 w_tail=w_tail, wg_col=gate_ops[0], wg_row=gate_ops[1],
                                 gb_col=gate_ops[2], gb_row=gate_ops[3])

        def stream(x2d, seq, row_fn, tm):
            p, v, o, g_col, g_row = proj(x2d, row_fn=row_fn, tm=tm, seq=seq)
            q, k = _qk_conv(p.reshape(bsz, seq, P_DIM), qk_conv_w[l])
            g_row = g_row.reshape(HEADS, SUBLANES, -1, CHUNK).transpose(0, 2, 1, 3)
            return p, (q, k, v, o, g_row, g_col)

        p_lat, lat_stream = stream(xl, t_lat, lat_row(tm_proj), tm_proj)
        p_ctx, ctx_stream = stream(xc, t_ctx, ctx_row, min(tm_proj, t_ctx))
        hm = _mlstm(ctx_stream, lat_stream, row2(mlstm_norm_g[l]), ctx_out=not last)

        mix_ffn = functools.partial(
            _mix_ffn, mods=mods, dw_w=conv_dw_w[l], ln_g=row2(conv_ln_g[l]), ln_b=row2(conv_ln_b[l]),
            w_conv=w_conv, w_m=w_m, w_out=w_o, g2=row2(norm2_g[l]), w1=w1, w2=w2,
            final_g=row2(final_g), tm=tm_mix)
        xl = mix_ffn(p_lat, hm[0], xl, row_fn=lat_row(tm_mix), group=GRID_W, final_norm=last)
        if not last:
            xc = mix_ffn(p_ctx, hm[1], xc, row_fn=ctx_row, group=t_ctx, final_norm=False)
    return xl.reshape(bsz, t_lat, D_MODEL)
```

```python
import functools

import jax
import jax.numpy as jnp
from jax import lax
from jax.experimental import pallas as pl
from jax.experimental.pallas import tpu as pltpu

D_MODEL = 1024
GRID_W = 64
CONV_WIDTH = 31
CONV_HALO = 16
CONV_TOK_BLK = 16
SUBLANES = 8
BF16_TILE_ROWS = 16
HEADS = 4
MLSTM_DIM = 2 * D_MODEL
HEAD_DIM = MLSTM_DIM // HEADS
QK_CONV_WIDTH = 5
CHUNK = 256
D_FF = 4 * D_MODEL
N_MOD = 6
EPS = 1e-6
M_INIT = -1e30
N_GATE = 4
LANES = 128
GATE_COLS = HEADS * LANES
GATE_ROWS = HEADS * SUBLANES
W_BLK = 2 * D_MODEL
N_MAIN_BLKS = 5
FF_BLK = D_MODEL
P_DIM = 12 * D_MODEL
COL_A, COL_B, COL_GC, COL_GM = 0, 1, 10, 11
COL_Q, COL_K, COL_V, COL_O = 4, 8, 12, 16
MOD_ROWS = 16
VMEM_LIMIT = 56 * 1024 * 1024

BF16 = jnp.bfloat16
F32 = jnp.float32


def _silu(x):
    return x * jax.nn.sigmoid(x)


def _bdot(a, b):
    return jnp.dot(a, b, preferred_element_type=F32)


def _cparams(*sem):
    return pltpu.CompilerParams(dimension_semantics=sem, vmem_limit_bytes=VMEM_LIMIT)


def _mod_kernel(cc_ref, w_ref, b_ref, o_ref):
    s = _silu(cc_ref[...]).astype(BF16)
    o_ref[...] = _bdot(s, w_ref[...].astype(BF16)) + b_ref[...]


def _modulation(cc, w_mod, b_mod):
    depth = w_mod.shape[0]
    tn = D_MODEL
    return pl.pallas_call(
        _mod_kernel,
        grid=(depth, N_MOD * D_MODEL // tn),
        in_specs=[
            pl.BlockSpec((MOD_ROWS, D_MODEL), lambda l, j: (0, 0)),
            pl.BlockSpec((None, D_MODEL, tn), lambda l, j: (l, 0, j)),
            pl.BlockSpec((None, 1, tn), lambda l, j: (l, 0, j)),
        ],
        out_specs=pl.BlockSpec((None, MOD_ROWS, tn), lambda l, j: (l, 0, j)),
        out_shape=jax.ShapeDtypeStruct((depth, MOD_ROWS, N_MOD * D_MODEL), F32),
        compiler_params=_cparams("arbitrary", "arbitrary"),
        name="modulation",
    )(cc, w_mod, b_mod.reshape(depth, 1, N_MOD * D_MODEL))


def _mod_spec(k, row_fn):
    return pl.BlockSpec((None, 1, D_MODEL), lambda *idx: (row_fn(*idx), 0, k))


def _rms_modulate(x, g, shift, scale):
    y = x * lax.rsqrt(jnp.mean(x * x, axis=-1, keepdims=True) + EPS) * g
    return y * (1.0 + scale) + shift


def _in_proj_kernel(x_ref, sh_ref, sc_ref, g_ref, w_ref, wt_ref, wgc_ref, wgr_ref, gbc_ref, gbr_ref,
                    p_ref, gcol_ref, grow_ref, h_scr):
    j = pl.program_id(1)

    @pl.when(j == 0)
    def _():
        h = _rms_modulate(x_ref[...], g_ref[...], sh_ref[...], sc_ref[...]).astype(BF16)
        h_scr[...] = h
        gcol_ref[...] = _bdot(h, wgc_ref[...]) + gbc_ref[...]
        grow_ref[...] = lax.dot_general(wgr_ref[...], h, (((1,), (1,)), ((), ())),
                                        preferred_element_type=F32) + gbr_ref[...]

    @pl.when(j < N_MAIN_BLKS)
    def _():
        p_ref[...] = _bdot(h_scr[...], w_ref[...]).astype(BF16)

    @pl.when(j == N_MAIN_BLKS)
    def _():
        p_ref[...] = _bdot(h_scr[...], wt_ref[...]).astype(BF16)


def _in_proj(x, mods, g, w_in, layer, w_tail, wg_col, wg_row, gb_col, gb_row, row_fn, tm):
    n = x.shape[0]
    const = lambda shape: pl.BlockSpec(shape, lambda i, j: (0, 0), pipeline_mode=pl.Buffered(1))
    return pl.pallas_call(
        _in_proj_kernel,
        grid=(n // tm, N_MAIN_BLKS + 1),
        in_specs=[
            pl.BlockSpec((tm, D_MODEL), lambda i, j: (i, 0)),
            _mod_spec(0, lambda i, j: row_fn(i)),
            _mod_spec(1, lambda i, j: row_fn(i)),
            const((1, D_MODEL)),
            pl.BlockSpec((None, D_MODEL, W_BLK),
                         lambda i, j: (layer, 0, jnp.minimum(j, N_MAIN_BLKS - 1))),
            const((D_MODEL, W_BLK)),
            const((D_MODEL, GATE_COLS)), const((GATE_ROWS, D_MODEL)),
            const((1, GATE_COLS)), const((GATE_ROWS, 1)),
        ],
        out_specs=[
            pl.BlockSpec((tm, W_BLK), lambda i, j: (i, j)),
            pl.BlockSpec((tm, GATE_COLS), lambda i, j: (i, 0)),
            pl.BlockSpec((GATE_ROWS, tm), lambda i, j: (0, i)),
        ],
        out_shape=[
            jax.ShapeDtypeStruct((n, P_DIM), BF16),
            jax.ShapeDtypeStruct((n, GATE_COLS), F32),
            jax.ShapeDtypeStruct((GATE_ROWS, n), F32),
        ],
        scratch_shapes=[pltpu.VMEM((tm, D_MODEL), BF16)],
        compiler_params=_cparams("arbitrary", "arbitrary"),
        name="in_proj",
    )(x, mods, mods, g, w_in, w_tail, wg_col, wg_row, gb_col, gb_row)


def _qk_conv_kernel(q_ref, k_ref, wq_ref, wk_ref, qo_ref, ko_ref):
    t = q_ref.shape[0]
    half = QK_CONV_WIDTH // 2
    edge = 2 * BF16_TILE_ROWS

    def taps(x, w_ref, valid_fn):
        n = x.shape[0]
        acc = x * w_ref[half:half + 1, :]
        for off in range(-half, half + 1):
            if off == 0:
                continue
            shifted = pltpu.roll(x, (-off) % n, 0)
            if valid_fn is not None:
                shifted = jnp.where(valid_fn(off), shifted, 0.0)
            acc = acc + shifted * w_ref[off + half:off + half + 1, :]
        return acc

    def conv(x_ref, w_ref, o_ref, scale):
        def out(acc):
            y = _silu(acc)
            return (y if scale is None else y * scale).astype(BF16)

        o_ref[...] = out(taps(x_ref[...].astype(F32), w_ref, None))
        row = lax.broadcasted_iota(jnp.int32, (edge, 1), 0)
        top = taps(x_ref[0:edge, :].astype(F32), w_ref, lambda off: row + off >= 0)
        o_ref[0:BF16_TILE_ROWS, :] = out(top[0:BF16_TILE_ROWS, :])
        bot = taps(x_ref[t - edge:t, :].astype(F32), w_ref, lambda off: row + off < edge)
        o_ref[t - BF16_TILE_ROWS:t, :] = out(bot[edge - BF16_TILE_ROWS:edge, :])

    conv(q_ref, wq_ref, qo_ref, None)
    conv(k_ref, wk_ref, ko_ref, HEAD_DIM ** -0.5)


def _qk_conv(p, qk_w):
    bsz, t, _ = p.shape
    out = jax.ShapeDtypeStruct((bsz, t, MLSTM_DIM), BF16)
    blk = lambda c0: pl.BlockSpec((None, t, HEAD_DIM), lambda b, c: (b, 0, c0 + c))
    wblk = lambda c0: pl.BlockSpec((QK_CONV_WIDTH, HEAD_DIM), lambda b, c: (0, c0 + c))
    return pl.pallas_call(
        _qk_conv_kernel,
        grid=(bsz, HEADS),
        in_specs=[blk(COL_Q), blk(COL_K), wblk(0), wblk(HEADS)],
        out_specs=[blk(0), blk(0)],
        out_shape=[out, out],
        compiler_params=_cparams("arbitrary", "arbitrary"),
        name="qk_conv",
    )(p, p, qk_w, qk_w)


def _mlstm_chunk(q, k, v, g_row, g_col, state, reverse):
    c_ref, c16_ref, n_ref, m_ref = state
    gi, gf = (2, 3) if reverse else (0, 1)
    i_row, i_col = g_row[gi:gi + 1, :], g_col[:, gi:gi + 1]
    lf_row = jax.nn.log_sigmoid(g_row[gf:gf + 1, :])
    lf_col = jax.nn.log_sigmoid(g_col[:, gf:gf + 1])
    r = lax.broadcasted_iota(jnp.int32, (CHUNK, CHUNK), 0)
    c = lax.broadcasted_iota(jnp.int32, (CHUNK, CHUNK), 1)
    mask = (c >= r) if reverse else (c <= r)
    mask_t = (r >= c) if reverse else (r <= c)
    b_col = jnp.sum(jnp.where(mask, lf_row, 0.0), axis=1, keepdims=True)
    b_row = jnp.sum(jnp.where(mask_t, lf_col, 0.0), axis=0, keepdims=True)
    b_last = jnp.sum(lf_row, axis=1, keepdims=True)
    m_st = m_ref[...]

    log_d = jnp.where(mask, b_col - b_row + i_row, -jnp.inf)
    inter = b_col + m_st
    m_t = jnp.maximum(inter, jnp.max(log_d, axis=1, keepdims=True))
    w_intra = jnp.exp(log_d - m_t)
    w_inter = jnp.exp(inter - m_t)
    s = lax.dot_general(q, k, (((1,), (1,)), ((), ())), preferred_element_type=F32) * w_intra
    num = w_inter * _bdot(q, c16_ref[...]) + _bdot(s.astype(BF16), v)
    n16 = jnp.broadcast_to(n_ref[...], (SUBLANES, HEAD_DIM)).astype(BF16)
    qn = lax.dot_general(q, n16, (((1,), (1,)), ((), ())), preferred_element_type=F32)[:, 0:1]
    den = w_inter * qn + jnp.sum(s, axis=1, keepdims=True)
    h = num / jnp.maximum(jnp.abs(den), jnp.exp(-m_t))

    g_end = b_last - b_col + i_col
    inter_end = b_last + m_st
    m_new = jnp.maximum(inter_end, jnp.max(g_end, axis=0, keepdims=True))
    ws = jnp.exp(g_end - m_new)
    we = jnp.exp(inter_end - m_new)
    kw = k.astype(F32) * ws
    kv = lax.dot_general(kw.astype(BF16), v, (((0,), (0,)), ((), ())), preferred_element_type=F32)
    c_new = we * c_ref[...] + kv
    c_ref[...] = c_new
    c16_ref[...] = c_new.astype(BF16)
    n_ref[...] = we * n_ref[...] + jnp.sum(kw, axis=0, keepdims=True)
    m_ref[...] = m_new
    return h


def _mlstm_kernel(qc_ref, kc_ref, vc_ref, oc_ref, growc_ref, gcolc_ref,
                  ql_ref, kl_ref, vl_ref, ol_ref, growl_ref, gcoll_ref, ng_ref, *rest, ctx_out):
    if ctx_out:
        hl_ref, hc_ref = rest[:2]
        scratch = rest[2:]
        accc_ref = scratch[9]
    else:
        hl_ref, hc_ref, accc_ref = rest[0], None, None
        scratch = rest[1:]
    state_f, state_b, accl_ref = scratch[0:4], scratch[4:8], scratch[8]
    nc_ctx = qc_ref.shape[0] // CHUNK
    nc_lat = ql_ref.shape[0] // CHUNK
    assert (nc_ctx == 1 or nc_ctx % 2 == 0) and nc_lat % 2 == 0

    for c_ref, c16_ref, n_ref, m_ref in (state_f, state_b):
        c_ref[...] = jnp.zeros_like(c_ref)
        c16_ref[...] = jnp.zeros_like(c16_ref)
        n_ref[...] = jnp.zeros_like(n_ref)
        m_ref[...] = jnp.full_like(m_ref, M_INIT)

    def finish(h, o_ref, out_ref, rows):
        hn = h * lax.rsqrt(jnp.mean(h * h, axis=-1, keepdims=True) + EPS) * ng_ref[...]
        out_ref[rows, :] = (hn * jax.nn.sigmoid(o_ref[rows, :].astype(F32))).astype(BF16)

    def pair_step(i, n_chunks, refs, o_ref, out_ref, acc_ref, closing):
        q_ref, k_ref, v_ref, grow_ref, gcol_ref = refs
        steps = ((i, state_f, False, closing[0]), (n_chunks - 1 - i, state_b, True, closing[1]))
        for j, state, reverse, close in steps:
            if isinstance(j, int):
                rows = pl.ds(j * CHUNK, CHUNK)
            else:
                rows = pl.ds(pl.multiple_of(j * CHUNK, CHUNK), CHUNK)
            h = _mlstm_chunk(q_ref[rows, :], k_ref[rows, :], v_ref[rows, :],
                             grow_ref[j], gcol_ref[rows, :], state, reverse)
            if out_ref is None:
                continue
            if close:
                finish(acc_ref[rows, :] + h, o_ref, out_ref, rows)
            else:
                acc_ref[rows, :] = h

    ctx = ((qc_ref, kc_ref, vc_ref, growc_ref, gcolc_ref), oc_ref, hc_ref, accc_ref)
    lat = ((ql_ref, kl_ref, vl_ref, growl_ref, gcoll_ref), ol_ref, hl_ref, accl_ref)
    for n_chunks, args in ((nc_ctx, ctx), (nc_lat, lat)):
        half = n_chunks // 2

        def opening(i, carry, n_chunks=n_chunks, args=args):
            pair_step(i, n_chunks, *args, closing=(False, False))
            return carry

        def closing(i, carry, n_chunks=n_chunks, args=args):
            pair_step(i, n_chunks, *args, closing=(True, True))
            return carry

        if n_chunks == 1:
            pair_step(0, 1, *args, closing=(False, True))
        elif half == 1:
            opening(0, 0)
            closing(1, 0)
        else:
            lax.fori_loop(0, half, opening, 0)
            lax.fori_loop(half, n_chunks, closing, 0)


def _mlstm(qc, kc, p_ctx, gates_ctx, ql, kl, p_lat, gates_lat, norm_g, ctx_out):
    bsz, t_ctx, _ = qc.shape
    t_lat = ql.shape[1]
    blk = lambda t, c0: pl.BlockSpec((None, t, HEAD_DIM), lambda b, h: (b, 0, c0 + h))

    def stream_specs(t):
        return [
            blk(t, 0), blk(t, 0), blk(t, COL_V), blk(t, COL_O),
            pl.BlockSpec((None, t // CHUNK, SUBLANES, CHUNK), lambda b, h: (h, b, 0, 0)),
            pl.BlockSpec((None, t, LANES), lambda b, h: (b, 0, h)),
        ]

    in_specs = (stream_specs(t_ctx) + stream_specs(t_lat)
                + [pl.BlockSpec((1, HEAD_DIM), lambda b, h: (0, h))])
    out_specs = [blk(t_lat, 0)]
    out_shape = [jax.ShapeDtypeStruct((bsz, t_lat, MLSTM_DIM), BF16)]
    state = [pltpu.VMEM((HEAD_DIM, HEAD_DIM), F32), pltpu.VMEM((HEAD_DIM, HEAD_DIM), BF16),
             pltpu.VMEM((1, HEAD_DIM), F32), pltpu.VMEM((1, 1), F32)]
    scratch = state + state + [pltpu.VMEM((t_lat, HEAD_DIM), F32)]
    if ctx_out:
        out_specs.append(blk(t_ctx, 0))
        out_shape.append(jax.ShapeDtypeStruct((bsz, t_ctx, MLSTM_DIM), BF16))
        scratch.append(pltpu.VMEM((t_ctx, HEAD_DIM), F32))
    return pl.pallas_call(
        functools.partial(_mlstm_kernel, ctx_out=ctx_out),
        grid=(bsz, HEADS),
        in_specs=in_specs,
        out_specs=out_specs,
        out_shape=out_shape,
        scratch_shapes=scratch,
        compiler_params=_cparams("arbitrary", "arbitrary"),
        name="mlstm",
    )(qc, kc, p_ctx, p_ctx, *gates_ctx, ql, kl, p_lat, p_lat, *gates_lat, norm_g)


def _mix_ffn_kernel(a_ref, b_ref, gc_ref, gm_ref, hm_ref, x_ref, ga1_ref, sh2_ref, sc2_ref, ga2_ref,
                    dw_ref, lng_ref, lnb_ref, wc_ref, wm_ref, wo_ref, g2_ref, w1_ref, w2_ref, fg_ref,
                    o_ref, pad_ref, cv_ref, *, group, final_norm):
    tm = a_ref.shape[0]
    stride = group + 2 * CONV_HALO
    u = a_ref[...].astype(F32) * jax.nn.sigmoid(b_ref[...].astype(F32))
    u3 = pltpu.einshape("t(sl)->tsl", u, s=SUBLANES, l=LANES)
    halo = jnp.zeros((CONV_HALO, SUBLANES, LANES), F32)
    for g in range(tm // group):
        base = g * stride
        pad_ref[base:base + CONV_HALO] = halo
        pad_ref[base + CONV_HALO:base + CONV_HALO + group] = u3[g * group:(g + 1) * group]
        pad_ref[base + CONV_HALO + group:base + stride] = halo
    first = CONV_HALO - CONV_WIDTH // 2
    for g in range(tm // group):
        for tb in range(group // CONV_TOK_BLK):
            t0 = tb * CONV_TOK_BLK
            acc = jnp.zeros((CONV_TOK_BLK, SUBLANES, LANES), F32)
            for j in range(CONV_WIDTH):
                start = g * stride + first + j + t0
                acc = acc + pad_ref[start:start + CONV_TOK_BLK] * dw_ref[j]
            cv_ref[g * group + t0:g * group + t0 + CONV_TOK_BLK] = acc
    cv = pltpu.einshape("tsl->t(sl)", cv_ref[...])
    mu = jnp.mean(cv, axis=-1, keepdims=True)
    var = jnp.mean(jnp.square(cv - mu), axis=-1, keepdims=True)
    ln = (cv - mu) * lax.rsqrt(var + EPS) * lng_ref[...] + lnb_ref[...]
    y_conv = _bdot(_silu(ln).astype(BF16), wc_ref[...])
    y_m = _bdot(hm_ref[...], wm_ref[...])
    y = (jax.nn.sigmoid(gc_ref[...].astype(F32)) * y_conv
         + jax.nn.sigmoid(gm_ref[...].astype(F32)) * y_m)
    x1 = x_ref[...] + ga1_ref[...] * _bdot(y.astype(BF16), wo_ref[...])

    h2 = _rms_modulate(x1, g2_ref[...], sh2_ref[...], sc2_ref[...]).astype(BF16)
    acc = jnp.zeros(x1.shape, F32)
    for cb in range(D_FF // FF_BLK):
        cols = slice(cb * FF_BLK, (cb + 1) * FF_BLK)
        t = jnp.square(jnp.maximum(_bdot(h2, w1_ref[:, cols]), 0.0))
        acc = acc + _bdot(t.astype(BF16), w2_ref[cols, :])
    x2 = x1 + ga2_ref[...] * acc
    if final_norm:
        x2 = x2 * lax.rsqrt(jnp.mean(x2 * x2, axis=-1, keepdims=True) + EPS) * fg_ref[...]
    o_ref[...] = x2


def _mix_ffn(p, hm, x, mods, dw_w, ln_g, ln_b, w_conv, w_m, w_out, g2, w1, w2, final_g,
             row_fn, tm, group, final_norm):
    n = x.shape[0]
    pad_rows = tm // group * (group + 2 * CONV_HALO)
    const = lambda shape: pl.BlockSpec(shape, lambda i: (0, 0), pipeline_mode=pl.Buffered(1))
    pblk = lambda c: pl.BlockSpec((tm, D_MODEL), lambda i: (i, c))
    mod = lambda k: _mod_spec(k, row_fn)
    return pl.pallas_call(
        functools.partial(_mix_ffn_kernel, group=group, final_norm=final_norm),
        grid=(n // tm,),
        in_specs=[
            pblk(COL_A), pblk(COL_B), pblk(COL_GC), pblk(COL_GM),
            pl.BlockSpec((tm, MLSTM_DIM), lambda i: (i, 0)),
            pl.BlockSpec((tm, D_MODEL), lambda i: (i, 0)),
            mod(2), mod(3), mod(4), mod(5),
            pl.BlockSpec((CONV_WIDTH, SUBLANES, LANES), lambda i: (0, 0, 0),
                         pipeline_mode=pl.Buffered(1)),
            const((1, D_MODEL)), const((1, D_MODEL)),
            const((D_MODEL, D_MODEL)), const((MLSTM_DIM, D_MODEL)), const((D_MODEL, D_MODEL)),
            const((1, D_MODEL)), const((D_MODEL, D_FF)), const((D_FF, D_MODEL)), const((1, D_MODEL)),
        ],
        out_specs=pl.BlockSpec((tm, D_MODEL), lambda i: (i, 0)),
        out_shape=jax.ShapeDtypeStruct((n, D_MODEL), F32),
        scratch_shapes=[
            pltpu.VMEM((pad_rows, SUBLANES, LANES), F32),
            pltpu.VMEM((tm, SUBLANES, LANES), F32),
        ],
        compiler_params=_cparams("arbitrary"),
        name="mix_ffn",
    )(p, p, p, p, hm, x, mods, mods, mods, mods, dw_w, ln_g, ln_b, w_conv, w_m, w_out, g2, w1, w2,
      final_g)


def _gate_weights(w_g, b_g):
    w = w_g.reshape(D_MODEL, N_GATE, HEADS).transpose(0, 2, 1)
    b = b_g.reshape(N_GATE, HEADS).T
    pad_to = lambda a, width: jnp.pad(a, [(0, 0)] * (a.ndim - 1) + [(0, width - N_GATE)])
    wg_col = pad_to(w, LANES).reshape(D_MODEL, GATE_COLS).astype(BF16)
    wg_row = pad_to(w, SUBLANES).reshape(D_MODEL, GATE_ROWS).T.astype(BF16)
    return wg_col, wg_row, pad_to(b, LANES).reshape(1, GATE_COLS), pad_to(b, SUBLANES).reshape(GATE_ROWS, 1)


def _gate_views(g_col, g_row, bsz):
    n = g_col.shape[0]
    rows = g_row.reshape(HEADS, SUBLANES, n // CHUNK, CHUNK).transpose(0, 2, 1, 3)
    return rows, g_col.reshape(bsz, n // bsz, GATE_COLS)


def _token_tiles(t_lat, n_ctx):
    tm_proj, tm_mix = 1024, 256
    assert t_lat % tm_proj == 0 and n_ctx % tm_proj == 0
    assert t_lat % tm_mix == 0 and tm_mix % GRID_W == 0 and n_ctx % tm_mix == 0
    return tm_proj, tm_mix


def kernel(x, c, ctx, c_ctx, w_mod, b_mod, norm1_g, w_in, mlstm_gate_b, qk_conv_w, conv_dw_w,
           conv_ln_g, conv_ln_b, w_conv_out, mlstm_norm_g, w_mlstm_out, w_out, norm2_g,
           w_ff1, w_ff2, final_g):
    bsz, t_lat, _ = x.shape
    t_ctx = ctx.shape[1]
    depth = w_mod.shape[0]
    assert bsz < MOD_ROWS and t_lat % GRID_W == 0

    cc = jnp.zeros((MOD_ROWS, D_MODEL), F32).at[:bsz].set(c).at[bsz].set(c_ctx)
    mod_all = _modulation(cc, w_mod, b_mod)

    g_lo = N_MAIN_BLKS * W_BLK
    g_hi = g_lo + N_GATE * HEADS
    row2 = lambda a: a.reshape(1, -1)

    tm_proj, tm_mix = _token_tiles(t_lat, bsz * t_ctx)
    lat_row = lambda tm: (lambda i: i // (t_lat // tm))
    ctx_row = lambda i: bsz

    w_in16 = w_in.astype(BF16)
    xl = x.reshape(bsz * t_lat, D_MODEL)
    xc = ctx.reshape(bsz * t_ctx, D_MODEL)
    for l in range(depth):
        last = l == depth - 1
        mods = mod_all[l].reshape(MOD_ROWS, 1, N_MOD * D_MODEL)
        w_tail = w_in16[l, :, g_hi:]
        gate_ops = _gate_weights(w_in[l, :, g_lo:g_hi], mlstm_gate_b[l])
        w_conv, w_m, w_o = (w_conv_out[l].astype(BF16), w_mlstm_out[l].astype(BF16),
                            w_out[l].astype(BF16))
        w1, w2 = w_ff1[l].astype(BF16), w_ff2[l].astype(BF16)

        proj = functools.partial(_in_proj, mods=mods, g=row2(norm1_g[l]), w_in=w_in16, layer=l,
                                 w_tail=w_tail, wg_col=gate_ops[0], wg_row=gate_ops[1],
                                 gb_col=gate_ops[2], gb_row=gate_ops[3], tm=tm_proj)
        p_lat, gcol_lat, grow_lat = proj(xl, row_fn=lat_row(tm_proj))
        p_ctx, gcol_ctx, grow_ctx = proj(xc, row_fn=ctx_row)
        p_lat3 = p_lat.reshape(bsz, t_lat, P_DIM)
        p_ctx3 = p_ctx.reshape(bsz, t_ctx, P_DIM)
        ql, kl = _qk_conv(p_lat3, qk_conv_w[l])
        qc, kc = _qk_conv(p_ctx3, qk_conv_w[l])
        hm = _mlstm(qc, kc, p_ctx3, _gate_views(gcol_ctx, grow_ctx, bsz),
                    ql, kl, p_lat3, _gate_views(gcol_lat, grow_lat, bsz),
                    row2(mlstm_norm_g[l]), ctx_out=not last)

        mix_ffn = functools.partial(
            _mix_ffn, mods=mods, dw_w=conv_dw_w[l].reshape(CONV_WIDTH, SUBLANES, LANES),
            ln_g=row2(conv_ln_g[l]), ln_b=row2(conv_ln_b[l]),
            w_conv=w_conv, w_m=w_m, w_out=w_o, g2=row2(norm2_g[l]), w1=w1, w2=w2,
            final_g=row2(final_g), tm=tm_mix)
        xl = mix_ffn(p_lat, hm[0].reshape(bsz * t_lat, MLSTM_DIM), xl, row_fn=lat_row(tm_mix),
                     group=GRID_W, final_norm=last)
        if not last:
            xc = mix_ffn(p_ctx, hm[1].reshape(bsz * t_ctx, MLSTM_DIM), xc, row_fn=ctx_row,
                         group=t_ctx, final_norm=False)
    return xl.reshape(bsz, t_lat, D_MODEL)
```

```python
import functools

import jax
import jax.numpy as jnp
from jax import lax
from jax.experimental import pallas as pl
from jax.experimental.pallas import tpu as pltpu

D_MODEL = 1024
GRID_W = 64
CONV_WIDTH = 31
CONV_HALO = 16
CONV_TOK_BLK = 16
SUBLANES = 8
BF16_TILE_ROWS = 16
HEADS = 4
MLSTM_DIM = 2 * D_MODEL
HEAD_DIM = MLSTM_DIM // HEADS
QK_CONV_WIDTH = 5
CHUNK = 256
D_FF = 4 * D_MODEL
N_MOD = 6
EPS = 1e-6
M_INIT = -1e30
N_GATE = 4
LANES = 128
GATE_COLS = HEADS * LANES
GATE_ROWS = HEADS * SUBLANES
W_BLK = 2 * D_MODEL
N_MAIN_BLKS = 5
FF_BLK = D_MODEL
P_DIM = 12 * D_MODEL
COL_A, COL_B, COL_GC, COL_GM = 0, 1, 10, 11
COL_Q, COL_K, COL_V, COL_O = 4, 8, 12, 16
MOD_ROWS = 16
VMEM_LIMIT = 56 * 1024 * 1024

BF16 = jnp.bfloat16
F32 = jnp.float32


def _silu(x):
    return x * jax.nn.sigmoid(x)


def _bdot(a, b):
    return jnp.dot(a, b, preferred_element_type=F32)


def _cparams(*sem):
    return pltpu.CompilerParams(dimension_semantics=sem, vmem_limit_bytes=VMEM_LIMIT)


def _mod_kernel(cc_ref, w_ref, b_ref, o_ref):
    s = _silu(cc_ref[...]).astype(BF16)
    o_ref[...] = _bdot(s, w_ref[...].astype(BF16)) + b_ref[...]


def _modulation(cc, w_mod, b_mod):
    depth = w_mod.shape[0]
    tn = D_MODEL
    return pl.pallas_call(
        _mod_kernel,
        grid=(depth, N_MOD * D_MODEL // tn),
        in_specs=[
            pl.BlockSpec((MOD_ROWS, D_MODEL), lambda l, j: (0, 0)),
            pl.BlockSpec((None, D_MODEL, tn), lambda l, j: (l, 0, j)),
            pl.BlockSpec((None, 1, tn), lambda l, j: (l, 0, j)),
        ],
        out_specs=pl.BlockSpec((None, MOD_ROWS, tn), lambda l, j: (l, 0, j)),
        out_shape=jax.ShapeDtypeStruct((depth, MOD_ROWS, N_MOD * D_MODEL), F32),
        compiler_params=_cparams("arbitrary", "arbitrary"),
        name="modulation",
    )(cc, w_mod, b_mod.reshape(depth, 1, N_MOD * D_MODEL))


def _mod_spec(k, row_fn):
    return pl.BlockSpec((None, 1, D_MODEL), lambda *idx: (row_fn(*idx), 0, k))


def _rms_modulate(x, g, shift, scale):
    y = x * lax.rsqrt(jnp.mean(x * x, axis=-1, keepdims=True) + EPS) * g
    return y * (1.0 + scale) + shift


def _in_proj_kernel(x_ref, sh_ref, sc_ref, g_ref, w_ref, wt_ref, wgc_ref, wgr_ref, gbc_ref, gbr_ref,
                    p_ref, gcol_ref, grow_ref, h_scr):
    blk = pl.program_id(1)

    @pl.when(blk == 0)
    def _():
        h = _rms_modulate(x_ref[...], g_ref[...], sh_ref[...], sc_ref[...]).astype(BF16)
        h_scr[...] = h
        gcol_ref[...] = _bdot(h, wgc_ref[...]) + gbc_ref[...]
        grow_ref[...] = lax.dot_general(wgr_ref[...], h, (((1,), (1,)), ((), ())),
                                        preferred_element_type=F32) + gbr_ref[...]

    @pl.when(blk < N_MAIN_BLKS)
    def _():
        p_ref[...] = _bdot(h_scr[...], w_ref[...]).astype(BF16)

    @pl.when(blk == N_MAIN_BLKS)
    def _():
        p_ref[...] = _bdot(h_scr[...], wt_ref[...]).astype(BF16)


def _in_proj(x, mods, g, w_in, layer, w_tail, wg_col, wg_row, gb_col, gb_row, row_fn, tm):
    n = x.shape[0]
    const = lambda shape: pl.BlockSpec(shape, lambda i, j: (0, 0), pipeline_mode=pl.Buffered(1))
    return pl.pallas_call(
        _in_proj_kernel,
        grid=(n // tm, N_MAIN_BLKS + 1),
        in_specs=[
            pl.BlockSpec((tm, D_MODEL), lambda i, j: (i, 0)),
            _mod_spec(0, lambda i, j: row_fn(i)),
            _mod_spec(1, lambda i, j: row_fn(i)),
            const((1, D_MODEL)),
            pl.BlockSpec((None, D_MODEL, W_BLK),
                         lambda i, j: (layer, 0, jnp.minimum(j, N_MAIN_BLKS - 1))),
            const((D_MODEL, W_BLK)),
            const((D_MODEL, GATE_COLS)), const((GATE_ROWS, D_MODEL)),
            const((1, GATE_COLS)), const((GATE_ROWS, 1)),
        ],
        out_specs=[
            pl.BlockSpec((tm, W_BLK), lambda i, j: (i, j)),
            pl.BlockSpec((tm, GATE_COLS), lambda i, j: (i, 0)),
            pl.BlockSpec((GATE_ROWS, tm), lambda i, j: (0, i)),
        ],
        out_shape=[
            jax.ShapeDtypeStruct((n, P_DIM), BF16),
            jax.ShapeDtypeStruct((n, GATE_COLS), F32),
            jax.ShapeDtypeStruct((GATE_ROWS, n), F32),
        ],
        scratch_shapes=[pltpu.VMEM((tm, D_MODEL), BF16)],
        compiler_params=_cparams("arbitrary", "arbitrary"),
        name="in_proj",
    )(x, mods, mods, g, w_in, w_tail, wg_col, wg_row, gb_col, gb_row)


def _qk_conv_kernel(q_ref, k_ref, wq_ref, wk_ref, qo_ref, ko_ref):
    t = q_ref.shape[0]
    half = QK_CONV_WIDTH // 2
    edge = 2 * BF16_TILE_ROWS

    def taps(x, w_ref, valid_fn):
        n = x.shape[0]
        acc = x * w_ref[half:half + 1, :]
        for off in range(-half, half + 1):
            if off == 0:
                continue
            shifted = pltpu.roll(x, (-off) % n, 0)
            if valid_fn is not None:
                shifted = jnp.where(valid_fn(off), shifted, 0.0)
            acc = acc + shifted * w_ref[off + half:off + half + 1, :]
        return acc

    def conv(x_ref, w_ref, o_ref, scale):
        def out(acc):
            y = _silu(acc)
            return (y if scale is None else y * scale).astype(BF16)

        o_ref[...] = out(taps(x_ref[...].astype(F32), w_ref, None))
        row = lax.broadcasted_iota(jnp.int32, (edge, 1), 0)
        top = taps(x_ref[0:edge, :].astype(F32), w_ref, lambda off: row + off >= 0)
        o_ref[0:BF16_TILE_ROWS, :] = out(top[0:BF16_TILE_ROWS, :])
        bot = taps(x_ref[t - edge:t, :].astype(F32), w_ref, lambda off: row + off < edge)
        o_ref[t - BF16_TILE_ROWS:t, :] = out(bot[edge - BF16_TILE_ROWS:edge, :])

    conv(q_ref, wq_ref, qo_ref, None)
    conv(k_ref, wk_ref, ko_ref, HEAD_DIM ** -0.5)


def _qk_conv(p, qk_w):
    bsz, t, _ = p.shape
    out = jax.ShapeDtypeStruct((bsz, t, MLSTM_DIM), BF16)
    blk = lambda c0: pl.BlockSpec((None, t, HEAD_DIM), lambda b, c: (b, 0, c0 + c))
    wblk = lambda c0: pl.BlockSpec((QK_CONV_WIDTH, HEAD_DIM), lambda b, c: (0, c0 + c))
    return pl.pallas_call(
        _qk_conv_kernel,
        grid=(bsz, HEADS),
        in_specs=[blk(COL_Q), blk(COL_K), wblk(0), wblk(HEADS)],
        out_specs=[blk(0), blk(0)],
        out_shape=[out, out],
        compiler_params=_cparams("arbitrary", "arbitrary"),
        name="qk_conv",
    )(p, p, qk_w, qk_w)


def _mlstm_chunk(q, k, v, g_row, g_col, state, reverse):
    c_ref, c16_ref, n_ref, m_ref = state
    gi, gf = (2, 3) if reverse else (0, 1)
    i_row, i_col = g_row[gi:gi + 1, :], g_col[:, gi:gi + 1]
    lf_row = jax.nn.log_sigmoid(g_row[gf:gf + 1, :])
    lf_col = jax.nn.log_sigmoid(g_col[:, gf:gf + 1])
    r = lax.broadcasted_iota(jnp.int32, (CHUNK, CHUNK), 0)
    c = lax.broadcasted_iota(jnp.int32, (CHUNK, CHUNK), 1)
    mask = (c >= r) if reverse else (c <= r)
    mask_t = (r >= c) if reverse else (r <= c)
    b_col = jnp.sum(jnp.where(mask, lf_row, 0.0), axis=1, keepdims=True)
    b_row = jnp.sum(jnp.where(mask_t, lf_col, 0.0), axis=0, keepdims=True)
    b_last = jnp.sum(lf_row, axis=1, keepdims=True)
    m_st = m_ref[...]

    log_d = jnp.where(mask, b_col - b_row + i_row, -jnp.inf)
    inter = b_col + m_st
    m_t = jnp.maximum(inter, jnp.max(log_d, axis=1, keepdims=True))
    w_intra = jnp.exp(log_d - m_t)
    w_inter = jnp.exp(inter - m_t)
    s = lax.dot_general(q, k, (((1,), (1,)), ((), ())), preferred_element_type=F32) * w_intra
    num = w_inter * _bdot(q, c16_ref[...]) + _bdot(s.astype(BF16), v)
    n16 = jnp.broadcast_to(n_ref[...], (SUBLANES, HEAD_DIM)).astype(BF16)
    qn = lax.dot_general(q, n16, (((1,), (1,)), ((), ())), preferred_element_type=F32)[:, 0:1]
    den = w_inter * qn + jnp.sum(s, axis=1, keepdims=True)
    h = num / jnp.maximum(jnp.abs(den), jnp.exp(-m_t))

    g_end = b_last - b_col + i_col
    inter_end = b_last + m_st
    m_new = jnp.maximum(inter_end, jnp.max(g_end, axis=0, keepdims=True))
    ws = jnp.exp(g_end - m_new)
    we = jnp.exp(inter_end - m_new)
    kw = k.astype(F32) * ws
    kv = lax.dot_general(kw.astype(BF16), v, (((0,), (0,)), ((), ())), preferred_element_type=F32)
    c_new = we * c_ref[...] + kv
    c_ref[...] = c_new
    c16_ref[...] = c_new.astype(BF16)
    n_ref[...] = we * n_ref[...] + jnp.sum(kw, axis=0, keepdims=True)
    m_ref[...] = m_new
    return h


def _mlstm_kernel(qc_ref, kc_ref, vc_ref, oc_ref, growc_ref, gcolc_ref,
                  ql_ref, kl_ref, vl_ref, ol_ref, growl_ref, gcoll_ref, ng_ref, *rest, ctx_out):
    if ctx_out:
        hl_ref, hc_ref = rest[:2]
        scratch = rest[2:]
        accc_ref = scratch[9]
    else:
        hl_ref, hc_ref, accc_ref = rest[0], None, None
        scratch = rest[1:]
    state_f, state_b, accl_ref = scratch[0:4], scratch[4:8], scratch[8]
    nc_ctx = qc_ref.shape[0] // CHUNK
    nc_lat = ql_ref.shape[0] // CHUNK
    assert (nc_ctx == 1 or nc_ctx % 2 == 0) and nc_lat % 2 == 0

    for c_ref, c16_ref, n_ref, m_ref in (state_f, state_b):
        c_ref[...] = jnp.zeros_like(c_ref)
        c16_ref[...] = jnp.zeros_like(c16_ref)
        n_ref[...] = jnp.zeros_like(n_ref)
        m_ref[...] = jnp.full_like(m_ref, M_INIT)

    def finish(h, o_ref, out_ref, rows):
        hn = h * lax.rsqrt(jnp.mean(h * h, axis=-1, keepdims=True) + EPS) * ng_ref[...]
        out_ref[rows, :] = (hn * jax.nn.sigmoid(o_ref[rows, :].astype(F32))).astype(BF16)

    def pair_step(i, n_chunks, refs, o_ref, out_ref, acc_ref, closing):
        q_ref, k_ref, v_ref, grow_ref, gcol_ref = refs
        steps = ((i, state_f, False, closing[0]), (n_chunks - 1 - i, state_b, True, closing[1]))
        for j, state, reverse, close in steps:
            if isinstance(j, int):
                rows = pl.ds(j * CHUNK, CHUNK)
            else:
                rows = pl.ds(pl.multiple_of(j * CHUNK, CHUNK), CHUNK)
            h = _mlstm_chunk(q_ref[rows, :], k_ref[rows, :], v_ref[rows, :],
                             grow_ref[j], gcol_ref[rows, :], state, reverse)
            if out_ref is None:
                continue
            if close:
                finish(acc_ref[rows, :] + h, o_ref, out_ref, rows)
            else:
                acc_ref[rows, :] = h

    ctx = ((qc_ref, kc_ref, vc_ref, growc_ref, gcolc_ref), oc_ref, hc_ref, accc_ref)
    lat = ((ql_ref, kl_ref, vl_ref, growl_ref, gcoll_ref), ol_ref, hl_ref, accl_ref)
    for n_chunks, args in ((nc_ctx, ctx), (nc_lat, lat)):
        half = n_chunks // 2

        def opening(i, carry, n_chunks=n_chunks, args=args):
            pair_step(i, n_chunks, *args, closing=(False, False))
            return carry

        def closing(i, carry, n_chunks=n_chunks, args=args):
            pair_step(i, n_chunks, *args, closing=(True, True))
            return carry

        if n_chunks == 1:
            pair_step(0, 1, *args, closing=(False, True))
        elif half == 1:
            opening(0, 0)
            closing(1, 0)
        else:
            lax.fori_loop(0, half, opening, 0)
            lax.fori_loop(half, n_chunks, closing, 0)


def _mlstm(qc, kc, p_ctx, gates_ctx, ql, kl, p_lat, gates_lat, norm_g, ctx_out):
    bsz, t_ctx, _ = qc.shape
    t_lat = ql.shape[1]
    blk = lambda t, c0: pl.BlockSpec((None, t, HEAD_DIM), lambda b, h: (b, 0, c0 + h))

    def stream_specs(t):
        return [
            blk(t, 0), blk(t, 0), blk(t, COL_V), blk(t, COL_O),
            pl.BlockSpec((None, t // CHUNK, SUBLANES, CHUNK), lambda b, h: (h, b, 0, 0)),
            pl.BlockSpec((None, t, LANES), lambda b, h: (b, 0, h)),
        ]

    in_specs = (stream_specs(t_ctx) + stream_specs(t_lat)
                + [pl.BlockSpec((1, HEAD_DIM), lambda b, h: (0, h))])
    out_specs = [blk(t_lat, 0)]
    out_shape = [jax.ShapeDtypeStruct((bsz, t_lat, MLSTM_DIM), BF16)]
    state = [pltpu.VMEM((HEAD_DIM, HEAD_DIM), F32), pltpu.VMEM((HEAD_DIM, HEAD_DIM), BF16),
             pltpu.VMEM((1, HEAD_DIM), F32), pltpu.VMEM((1, 1), F32)]
    scratch = state + state + [pltpu.VMEM((t_lat, HEAD_DIM), F32)]
    if ctx_out:
        out_specs.append(blk(t_ctx, 0))
        out_shape.append(jax.ShapeDtypeStruct((bsz, t_ctx, MLSTM_DIM), BF16))
        scratch.append(pltpu.VMEM((t_ctx, HEAD_DIM), F32))
    return pl.pallas_call(
        functools.partial(_mlstm_kernel, ctx_out=ctx_out),
        grid=(bsz, HEADS),
        in_specs=in_specs,
        out_specs=out_specs,
        out_shape=out_shape,
        scratch_shapes=scratch,
        compiler_params=_cparams("arbitrary", "arbitrary"),
        name="mlstm",
    )(qc, kc, p_ctx, p_ctx, *gates_ctx, ql, kl, p_lat, p_lat, *gates_lat, norm_g)


def _mix_ffn_kernel(a_ref, b_ref, gc_ref, gm_ref, hm_ref, x_ref, ga1_ref, sh2_ref, sc2_ref, ga2_ref,
                    dw_ref, lng_ref, lnb_ref, wc_ref, wm_ref, wo_ref, g2_ref, w1_ref, w2_ref, fg_ref,
                    o_ref, pad_ref, cv_ref, *, group, final_norm):
    tm = a_ref.shape[0]
    stride = group + 2 * CONV_HALO
    u = a_ref[...].astype(F32) * jax.nn.sigmoid(b_ref[...].astype(F32))
    u3 = pltpu.einshape("t(sl)->tsl", u, s=SUBLANES, l=LANES)
    halo = jnp.zeros((CONV_HALO, SUBLANES, LANES), F32)
    for g in range(tm // group):
        base = g * stride
        pad_ref[base:base + CONV_HALO] = halo
        pad_ref[base + CONV_HALO:base + CONV_HALO + group] = u3[g * group:(g + 1) * group]
        pad_ref[base + CONV_HALO + group:base + stride] = halo
    first = CONV_HALO - CONV_WIDTH // 2
    for g in range(tm // group):
        for tb in range(group // CONV_TOK_BLK):
            t0 = tb * CONV_TOK_BLK
            acc = jnp.zeros((CONV_TOK_BLK, SUBLANES, LANES), F32)
            for j in range(CONV_WIDTH):
                start = g * stride + first + j + t0
                acc = acc + pad_ref[start:start + CONV_TOK_BLK] * dw_ref[j]
            cv_ref[g * group + t0:g * group + t0 + CONV_TOK_BLK] = acc
    cv = pltpu.einshape("tsl->t(sl)", cv_ref[...])
    mu = jnp.mean(cv, axis=-1, keepdims=True)
    var = jnp.mean(jnp.square(cv - mu), axis=-1, keepdims=True)
    ln = (cv - mu) * lax.rsqrt(var + EPS) * lng_ref[...] + lnb_ref[...]
    y_conv = _bdot(_silu(ln).astype(BF16), wc_ref[...])
    y_m = _bdot(hm_ref[...], wm_ref[...])
    y = (jax.nn.sigmoid(gc_ref[...].astype(F32)) * y_conv
         + jax.nn.sigmoid(gm_ref[...].astype(F32)) * y_m)
    x1 = x_ref[...] + ga1_ref[...] * _bdot(y.astype(BF16), wo_ref[...])

    h2 = _rms_modulate(x1, g2_ref[...], sh2_ref[...], sc2_ref[...]).astype(BF16)
    acc = jnp.zeros(x1.shape, F32)
    for cb in range(D_FF // FF_BLK):
        cols = slice(cb * FF_BLK, (cb + 1) * FF_BLK)
        t = jnp.square(jnp.maximum(_bdot(h2, w1_ref[:, cols]), 0.0))
        acc = acc + _bdot(t.astype(BF16), w2_ref[cols, :])
    x2 = x1 + ga2_ref[...] * acc
    if final_norm:
        x2 = x2 * lax.rsqrt(jnp.mean(x2 * x2, axis=-1, keepdims=True) + EPS) * fg_ref[...]
    o_ref[...] = x2


def _mix_ffn(p, hm, x, mods, dw_w, ln_g, ln_b, w_conv, w_m, w_out, g2, w1, w2, final_g,
             row_fn, tm, group, final_norm):
    n = x.shape[0]
    pad_rows = tm // group * (group + 2 * CONV_HALO)
    const = lambda shape: pl.BlockSpec(shape, lambda i: (0, 0), pipeline_mode=pl.Buffered(1))
    pblk = lambda c: pl.BlockSpec((tm, D_MODEL), lambda i: (i, c))
    mod = lambda k: _mod_spec(k, row_fn)
    return pl.pallas_call(
        functools.partial(_mix_ffn_kernel, group=group, final_norm=final_norm),
        grid=(n // tm,),
        in_specs=[
            pblk(COL_A), pblk(COL_B), pblk(COL_GC), pblk(COL_GM),
            pl.BlockSpec((tm, MLSTM_DIM), lambda i: (i, 0)),
            pl.BlockSpec((tm, D_MODEL), lambda i: (i, 0)),
            mod(2), mod(3), mod(4), mod(5),
            pl.BlockSpec((CONV_WIDTH, SUBLANES, LANES), lambda i: (0, 0, 0),
                         pipeline_mode=pl.Buffered(1)),
            const((1, D_MODEL)), const((1, D_MODEL)),
            const((D_MODEL, D_MODEL)), const((MLSTM_DIM, D_MODEL)), const((D_MODEL, D_MODEL)),
            const((1, D_MODEL)), const((D_MODEL, D_FF)), const((D_FF, D_MODEL)), const((1, D_MODEL)),
        ],
        out_specs=pl.BlockSpec((tm, D_MODEL), lambda i: (i, 0)),
        out_shape=jax.ShapeDtypeStruct((n, D_MODEL), F32),
        scratch_shapes=[
            pltpu.VMEM((pad_rows, SUBLANES, LANES), F32),
            pltpu.VMEM((tm, SUBLANES, LANES), F32),
        ],
        compiler_params=_cparams("arbitrary"),
        name="mix_ffn",
    )(p, p, p, p, hm, x, mods, mods, mods, mods, dw_w, ln_g, ln_b, w_conv, w_m, w_out, g2, w1, w2,
      final_g)


def _gate_weights(w_g, b_g):
    w = w_g.reshape(D_MODEL, N_GATE, HEADS).transpose(0, 2, 1)
    b = b_g.reshape(N_GATE, HEADS).T
    pad_to = lambda a, width: jnp.pad(a, [(0, 0)] * (a.ndim - 1) + [(0, width - N_GATE)])
    wg_col = pad_to(w, LANES).reshape(D_MODEL, GATE_COLS).astype(BF16)
    wg_row = pad_to(w, SUBLANES).reshape(D_MODEL, GATE_ROWS).T.astype(BF16)
    return wg_col, wg_row, pad_to(b, LANES).reshape(1, GATE_COLS), pad_to(b, SUBLANES).reshape(GATE_ROWS, 1)


def _gate_views(g_col, g_row, bsz):
    n = g_col.shape[0]
    rows = g_row.reshape(HEADS, SUBLANES, n // CHUNK, CHUNK).transpose(0, 2, 1, 3)
    return rows, g_col.reshape(bsz, n // bsz, GATE_COLS)


def _token_tiles(t_lat, n_ctx):
    tm_proj, tm_mix = 1024, 512
    assert t_lat % tm_proj == 0 and n_ctx % tm_proj == 0
    assert t_lat % tm_mix == 0 and tm_mix % GRID_W == 0 and n_ctx % tm_mix == 0
    return tm_proj, tm_mix


def kernel(x, c, ctx, c_ctx, w_mod, b_mod, norm1_g, w_in, mlstm_gate_b, qk_conv_w, conv_dw_w,
           conv_ln_g, conv_ln_b, w_conv_out, mlstm_norm_g, w_mlstm_out, w_out, norm2_g,
           w_ff1, w_ff2, final_g):
    bsz, t_lat, _ = x.shape
    t_ctx = ctx.shape[1]
    depth = w_mod.shape[0]
    assert bsz < MOD_ROWS and t_lat % GRID_W == 0

    cc = jnp.zeros((MOD_ROWS, D_MODEL), F32).at[:bsz].set(c).at[bsz].set(c_ctx)
    mod_all = _modulation(cc, w_mod, b_mod)

    g_lo = N_MAIN_BLKS * W_BLK
    g_hi = g_lo + N_GATE * HEADS
    row2 = lambda a: a.reshape(1, -1)

    tm_proj, tm_mix = _token_tiles(t_lat, bsz * t_ctx)
    lat_row = lambda tm: (lambda i: i // (t_lat // tm))
    ctx_row = lambda i: bsz

    w_in16 = w_in.astype(BF16)
    xl = x.reshape(bsz * t_lat, D_MODEL)
    xc = ctx.reshape(bsz * t_ctx, D_MODEL)
    for l in range(depth):
        last = l == depth - 1
        mods = mod_all[l].reshape(MOD_ROWS, 1, N_MOD * D_MODEL)
        w_tail = w_in16[l, :, g_hi:]
        gate_ops = _gate_weights(w_in[l, :, g_lo:g_hi], mlstm_gate_b[l])
        w_conv, w_m, w_o = (w_conv_out[l].astype(BF16), w_mlstm_out[l].astype(BF16),
                            w_out[l].astype(BF16))
        w1, w2 = w_ff1[l].astype(BF16), w_ff2[l].astype(BF16)

        proj = functools.partial(_in_proj, mods=mods, g=row2(norm1_g[l]), w_in=w_in16, layer=l,
                                 w_tail=w_tail, wg_col=gate_ops[0], wg_row=gate_ops[1],
                                 gb_col=gate_ops[2], gb_row=gate_ops[3], tm=tm_proj)
        p_lat, gcol_lat, grow_lat = proj(xl, row_fn=lat_row(tm_proj))
        p_ctx, gcol_ctx, grow_ctx = proj(xc, row_fn=ctx_row)
        p_lat3 = p_lat.reshape(bsz, t_lat, P_DIM)
        p_ctx3 = p_ctx.reshape(bsz, t_ctx, P_DIM)
        ql, kl = _qk_conv(p_lat3, qk_conv_w[l])
        qc, kc = _qk_conv(p_ctx3, qk_conv_w[l])
        hm = _mlstm(qc, kc, p_ctx3, _gate_views(gcol_ctx, grow_ctx, bsz),
                    ql, kl, p_lat3, _gate_views(gcol_lat, grow_lat, bsz),
                    row2(mlstm_norm_g[l]), ctx_out=not last)

        mix_ffn = functools.partial(
            _mix_ffn, mods=mods, dw_w=conv_dw_w[l].reshape(CONV_WIDTH, SUBLANES, LANES),
            ln_g=row2(conv_ln_g[l]), ln_b=row2(conv_ln_b[l]),
            w_conv=w_conv, w_m=w_m, w_out=w_o, g2=row2(norm2_g[l]), w1=w1, w2=w2,
            final_g=row2(final_g), tm=tm_mix)
        xl = mix_ffn(p_lat, hm[0].reshape(bsz * t_lat, MLSTM_DIM), xl, row_fn=lat_row(tm_mix),
                     group=GRID_W, final_norm=last)
        if not last:
            xc = mix_ffn(p_ctx, hm[1].reshape(bsz * t_ctx, MLSTM_DIM), xc, row_fn=ctx_row,
                         group=t_ctx, final_norm=False)
    return xl.reshape(bsz, t_lat, D_MODEL)
```

```python
import functools

import jax
import jax.numpy as jnp
from jax import lax
from jax.experimental import pallas as pl
from jax.experimental.pallas import tpu as pltpu

D_MODEL = 1024
GRID_W = 64
CONV_WIDTH = 31
CONV_HALO = 16
CONV_TOK_BLK = 16
SUBLANES = 8
BF16_TILE_ROWS = 16
QK_TOK_BLK = 2 * BF16_TILE_ROWS
HEADS = 4
MLSTM_DIM = 2 * D_MODEL
HEAD_DIM = MLSTM_DIM // HEADS
QK_CONV_WIDTH = 5
CHUNK = 256
D_FF = 4 * D_MODEL
N_MOD = 6
EPS = 1e-6
M_INIT = -1e30
N_GATE = 4
LANES = 128
GATE_COLS = HEADS * LANES
GATE_ROWS = HEADS * SUBLANES
W_BLK = 2 * D_MODEL
N_MAIN_BLKS = 5
FF_BLK = D_MODEL
P_DIM = 12 * D_MODEL
COL_A, COL_B, COL_GC, COL_GM = 0, 1, 10, 11
COL_Q, COL_K, COL_V, COL_O = 4, 8, 12, 16
MOD_ROWS = 16
VMEM_LIMIT = 56 * 1024 * 1024

BF16 = jnp.bfloat16
F32 = jnp.float32


def _silu(x):
    return x * jax.nn.sigmoid(x)


def _bdot(a, b):
    return jnp.dot(a, b, preferred_element_type=F32)


def _cparams(*sem):
    return pltpu.CompilerParams(dimension_semantics=sem, vmem_limit_bytes=VMEM_LIMIT)


def _mod_kernel(cc_ref, w_ref, b_ref, o_ref):
    s = _silu(cc_ref[...]).astype(BF16)
    o_ref[...] = _bdot(s, w_ref[...].astype(BF16)) + b_ref[...]


def _modulation(cc, w_mod, b_mod):
    depth = w_mod.shape[0]
    tn = D_MODEL
    return pl.pallas_call(
        _mod_kernel,
        grid=(depth, N_MOD * D_MODEL // tn),
        in_specs=[
            pl.BlockSpec((MOD_ROWS, D_MODEL), lambda l, j: (0, 0)),
            pl.BlockSpec((None, D_MODEL, tn), lambda l, j: (l, 0, j)),
            pl.BlockSpec((None, 1, tn), lambda l, j: (l, 0, j)),
        ],
        out_specs=pl.BlockSpec((None, MOD_ROWS, tn), lambda l, j: (l, 0, j)),
        out_shape=jax.ShapeDtypeStruct((depth, MOD_ROWS, N_MOD * D_MODEL), F32),
        compiler_params=_cparams("arbitrary", "arbitrary"),
        name="modulation",
    )(cc, w_mod, b_mod.reshape(depth, 1, N_MOD * D_MODEL))


def _mod_spec(k, row_fn):
    return pl.BlockSpec((None, 1, D_MODEL), lambda *idx: (row_fn(*idx), 0, k))


def _rms_modulate(x, g, shift, scale):
    y = x * lax.rsqrt(jnp.mean(x * x, axis=-1, keepdims=True) + EPS) * g
    return y * (1.0 + scale) + shift


def _in_proj_kernel(x_ref, sh_ref, sc_ref, g_ref, w_ref, wt_ref, wgc_ref, wgr_ref, gbc_ref, gbr_ref,
                    p_ref, gcol_ref, grow_ref, h_scr):
    blk = pl.program_id(1)

    @pl.when(blk == 0)
    def _():
        h = _rms_modulate(x_ref[...], g_ref[...], sh_ref[...], sc_ref[...]).astype(BF16)
        h_scr[...] = h
        gcol_ref[...] = _bdot(h, wgc_ref[...]) + gbc_ref[...]
        grow_ref[...] = lax.dot_general(wgr_ref[...], h, (((1,), (1,)), ((), ())),
                                        preferred_element_type=F32) + gbr_ref[...]

    @pl.when(blk < N_MAIN_BLKS)
    def _():
        p_ref[...] = _bdot(h_scr[...], w_ref[...]).astype(BF16)

    @pl.when(blk == N_MAIN_BLKS)
    def _():
        p_ref[...] = _bdot(h_scr[...], wt_ref[...]).astype(BF16)


def _in_proj(x, mods, g, w_in, layer, w_tail, wg_col, wg_row, gb_col, gb_row, row_fn, tm):
    n = x.shape[0]
    const = lambda shape: pl.BlockSpec(shape, lambda i, j: (0, 0), pipeline_mode=pl.Buffered(1))
    return pl.pallas_call(
        _in_proj_kernel,
        grid=(n // tm, N_MAIN_BLKS + 1),
        in_specs=[
            pl.BlockSpec((tm, D_MODEL), lambda i, j: (i, 0)),
            _mod_spec(0, lambda i, j: row_fn(i)),
            _mod_spec(1, lambda i, j: row_fn(i)),
            const((1, D_MODEL)),
            pl.BlockSpec((None, D_MODEL, W_BLK),
                         lambda i, j: (layer, 0, jnp.minimum(j, N_MAIN_BLKS - 1))),
            const((D_MODEL, W_BLK)),
            const((D_MODEL, GATE_COLS)), const((GATE_ROWS, D_MODEL)),
            const((1, GATE_COLS)), const((GATE_ROWS, 1)),
        ],
        out_specs=[
            pl.BlockSpec((tm, W_BLK), lambda i, j: (i, j)),
            pl.BlockSpec((tm, GATE_COLS), lambda i, j: (i, 0)),
            pl.BlockSpec((GATE_ROWS, tm), lambda i, j: (0, i)),
        ],
        out_shape=[
            jax.ShapeDtypeStruct((n, P_DIM), BF16),
            jax.ShapeDtypeStruct((n, GATE_COLS), F32),
            jax.ShapeDtypeStruct((GATE_ROWS, n), F32),
        ],
        scratch_shapes=[pltpu.VMEM((tm, D_MODEL), BF16)],
        compiler_params=_cparams("arbitrary", "arbitrary"),
        name="in_proj",
    )(x, mods, mods, g, w_in, w_tail, wg_col, wg_row, gb_col, gb_row)


def _qk_conv_kernel(q_ref, k_ref, w_ref, qo_ref, ko_ref, pad_ref):
    t = q_ref.shape[0]
    half = QK_CONV_WIDTH // 2
    n_blk = t // QK_TOK_BLK
    zero = jnp.zeros((half, SUBLANES, LANES), F32)
    pad_ref[0:half] = zero
    pad_ref[half + t:half + t + half] = zero

    def rows_of(i):
        return pl.ds(pl.multiple_of(i * QK_TOK_BLK, QK_TOK_BLK), QK_TOK_BLK)

    def fill(i, carry):
        x = jnp.concatenate([q_ref[rows_of(i), :].astype(F32), k_ref[rows_of(i), :].astype(F32)], axis=1)
        pad_ref[pl.ds(half + i * QK_TOK_BLK, QK_TOK_BLK)] = pltpu.einshape(
            "t(sl)->tsl", x, s=SUBLANES, l=LANES)
        return carry

    lax.fori_loop(0, n_blk, fill, 0)

    def conv(i, carry):
        acc = pad_ref[pl.ds(i * QK_TOK_BLK, QK_TOK_BLK)] * w_ref[0]
        for j in range(1, QK_CONV_WIDTH):
            acc = acc + pad_ref[pl.ds(i * QK_TOK_BLK + j, QK_TOK_BLK)] * w_ref[j]
        y = pltpu.einshape("tsl->t(sl)", _silu(acc))
        qo_ref[rows_of(i), :] = y[:, :HEAD_DIM].astype(BF16)
        ko_ref[rows_of(i), :] = (y[:, HEAD_DIM:] * (HEAD_DIM ** -0.5)).astype(BF16)
        return carry

    lax.fori_loop(0, n_blk, conv, 0)


def _qk_conv(p, qk_w):
    bsz, t, _ = p.shape
    half = QK_CONV_WIDTH // 2
    out = jax.ShapeDtypeStruct((bsz, t, MLSTM_DIM), BF16)
    blk = lambda c0: pl.BlockSpec((None, t, HEAD_DIM), lambda b, c: (b, 0, c0 + c))
    w = jnp.concatenate([qk_w[:, :MLSTM_DIM].reshape(QK_CONV_WIDTH, HEADS, HEAD_DIM),
                         qk_w[:, MLSTM_DIM:].reshape(QK_CONV_WIDTH, HEADS, HEAD_DIM)], axis=2)
    w = w.transpose(1, 0, 2).reshape(HEADS, QK_CONV_WIDTH, SUBLANES, LANES)
    return pl.pallas_call(
        _qk_conv_kernel,
        grid=(bsz, HEADS),
        in_specs=[blk(COL_Q), blk(COL_K),
                  pl.BlockSpec((None, QK_CONV_WIDTH, SUBLANES, LANES), lambda b, c: (c, 0, 0, 0))],
        out_specs=[blk(0), blk(0)],
        out_shape=[out, out],
        scratch_shapes=[pltpu.VMEM((t + 2 * half, SUBLANES, LANES), F32)],
        compiler_params=_cparams("arbitrary", "arbitrary"),
        name="qk_conv",
    )(p, p, w)


def _mlstm_chunk(q, k, v, g_row, g_col, state, reverse):
    c_ref, c16_ref, n_ref, m_ref = state
    gi, gf = (2, 3) if reverse else (0, 1)
    i_row, i_col = g_row[gi:gi + 1, :], g_col[:, gi:gi + 1]
    lf_row = jax.nn.log_sigmoid(g_row[gf:gf + 1, :])
    lf_col = jax.nn.log_sigmoid(g_col[:, gf:gf + 1])
    r = lax.broadcasted_iota(jnp.int32, (CHUNK, CHUNK), 0)
    c = lax.broadcasted_iota(jnp.int32, (CHUNK, CHUNK), 1)
    mask = (c >= r) if reverse else (c <= r)
    mask_t = (r >= c) if reverse else (r <= c)
    b_col = jnp.sum(jnp.where(mask, lf_row, 0.0), axis=1, keepdims=True)
    b_row = jnp.sum(jnp.where(mask_t, lf_col, 0.0), axis=0, keepdims=True)
    b_last = jnp.sum(lf_row, axis=1, keepdims=True)
    m_st = m_ref[...]

    log_d = jnp.where(mask, b_col - b_row + i_row, -jnp.inf)
    inter = b_col + m_st
    m_t = jnp.maximum(inter, jnp.max(log_d, axis=1, keepdims=True))
    w_intra = jnp.exp(log_d - m_t)
    w_inter = jnp.exp(inter - m_t)
    s = lax.dot_general(q, k, (((1,), (1,)), ((), ())), preferred_element_type=F32) * w_intra
    num = w_inter * _bdot(q, c16_ref[...]) + _bdot(s.astype(BF16), v)
    n16 = jnp.broadcast_to(n_ref[...], (SUBLANES, HEAD_DIM)).astype(BF16)
    qn = lax.dot_general(q, n16, (((1,), (1,)), ((), ())), preferred_element_type=F32)[:, 0:1]
    den = w_inter * qn + jnp.sum(s, axis=1, keepdims=True)
    h = num / jnp.maximum(jnp.abs(den), jnp.exp(-m_t))

    g_end = b_last - b_col + i_col
    inter_end = b_last + m_st
    m_new = jnp.maximum(inter_end, jnp.max(g_end, axis=0, keepdims=True))
    ws = jnp.exp(g_end - m_new)
    we = jnp.exp(inter_end - m_new)
    kw = k.astype(F32) * ws
    kv = lax.dot_general(kw.astype(BF16), v, (((0,), (0,)), ((), ())), preferred_element_type=F32)
    c_new = we * c_ref[...] + kv
    c_ref[...] = c_new
    c16_ref[...] = c_new.astype(BF16)
    n_ref[...] = we * n_ref[...] + jnp.sum(kw, axis=0, keepdims=True)
    m_ref[...] = m_new
    return h


def _mlstm_kernel(qc_ref, kc_ref, vc_ref, oc_ref, growc_ref, gcolc_ref,
                  ql_ref, kl_ref, vl_ref, ol_ref, growl_ref, gcoll_ref, ng_ref, *rest, ctx_out):
    if ctx_out:
        hl_ref, hc_ref = rest[:2]
        scratch = rest[2:]
        accc_ref = scratch[9]
    else:
        hl_ref, hc_ref, accc_ref = rest[0], None, None
        scratch = rest[1:]
    state_f, state_b, accl_ref = scratch[0:4], scratch[4:8], scratch[8]
    nc_ctx = qc_ref.shape[0] // CHUNK
    nc_lat = ql_ref.shape[0] // CHUNK
    assert (nc_ctx == 1 or nc_ctx % 2 == 0) and nc_lat % 2 == 0

    for c_ref, c16_ref, n_ref, m_ref in (state_f, state_b):
        c_ref[...] = jnp.zeros_like(c_ref)
        c16_ref[...] = jnp.zeros_like(c16_ref)
        n_ref[...] = jnp.zeros_like(n_ref)
        m_ref[...] = jnp.full_like(m_ref, M_INIT)

    def finish(h, o_ref, out_ref, rows):
        hn = h * lax.rsqrt(jnp.mean(h * h, axis=-1, keepdims=True) + EPS) * ng_ref[...]
        out_ref[rows, :] = (hn * jax.nn.sigmoid(o_ref[rows, :].astype(F32))).astype(BF16)

    def pair_step(i, n_chunks, refs, o_ref, out_ref, acc_ref, closing):
        q_ref, k_ref, v_ref, grow_ref, gcol_ref = refs
        steps = ((i, state_f, False, closing[0]), (n_chunks - 1 - i, state_b, True, closing[1]))
        for j, state, reverse, close in steps:
            if isinstance(j, int):
                rows = pl.ds(j * CHUNK, CHUNK)
            else:
                rows = pl.ds(pl.multiple_of(j * CHUNK, CHUNK), CHUNK)
            h = _mlstm_chunk(q_ref[rows, :], k_ref[rows, :], v_ref[rows, :],
                             grow_ref[j], gcol_ref[rows, :], state, reverse)
            if out_ref is None:
                continue
            if close:
                finish(acc_ref[rows, :] + h, o_ref, out_ref, rows)
            else:
                acc_ref[rows, :] = h

    ctx = ((qc_ref, kc_ref, vc_ref, growc_ref, gcolc_ref), oc_ref, hc_ref, accc_ref)
    lat = ((ql_ref, kl_ref, vl_ref, growl_ref, gcoll_ref), ol_ref, hl_ref, accl_ref)
    for n_chunks, args in ((nc_ctx, ctx), (nc_lat, lat)):
        half = n_chunks // 2

        def opening(i, carry, n_chunks=n_chunks, args=args):
            pair_step(i, n_chunks, *args, closing=(False, False))
            return carry

        def closing(i, carry, n_chunks=n_chunks, args=args):
            pair_step(i, n_chunks, *args, closing=(True, True))
            return carry

        if n_chunks == 1:
            pair_step(0, 1, *args, closing=(False, True))
        elif half == 1:
            opening(0, 0)
            closing(1, 0)
        else:
            lax.fori_loop(0, half, opening, 0)
            lax.fori_loop(half, n_chunks, closing, 0)


def _mlstm(qc, kc, p_ctx, gates_ctx, ql, kl, p_lat, gates_lat, norm_g, ctx_out):
    bsz, t_ctx, _ = qc.shape
    t_lat = ql.shape[1]
    blk = lambda t, c0: pl.BlockSpec((None, t, HEAD_DIM), lambda b, h: (b, 0, c0 + h))

    def stream_specs(t):
        return [
            blk(t, 0), blk(t, 0), blk(t, COL_V), blk(t, COL_O),
            pl.BlockSpec((None, t // CHUNK, SUBLANES, CHUNK), lambda b, h: (h, b, 0, 0)),
            pl.BlockSpec((None, t, LANES), lambda b, h: (b, 0, h)),
        ]

    in_specs = (stream_specs(t_ctx) + stream_specs(t_lat)
                + [pl.BlockSpec((1, HEAD_DIM), lambda b, h: (0, h))])
    out_specs = [blk(t_lat, 0)]
    out_shape = [jax.ShapeDtypeStruct((bsz, t_lat, MLSTM_DIM), BF16)]
    state = [pltpu.VMEM((HEAD_DIM, HEAD_DIM), F32), pltpu.VMEM((HEAD_DIM, HEAD_DIM), BF16),
             pltpu.VMEM((1, HEAD_DIM), F32), pltpu.VMEM((1, 1), F32)]
    scratch = state + state + [pltpu.VMEM((t_lat, HEAD_DIM), F32)]
    if ctx_out:
        out_specs.append(blk(t_ctx, 0))
        out_shape.append(jax.ShapeDtypeStruct((bsz, t_ctx, MLSTM_DIM), BF16))
        scratch.append(pltpu.VMEM((t_ctx, HEAD_DIM), F32))
    return pl.pallas_call(
        functools.partial(_mlstm_kernel, ctx_out=ctx_out),
        grid=(bsz, HEADS),
        in_specs=in_specs,
        out_specs=out_specs,
        out_shape=out_shape,
        scratch_shapes=scratch,
        compiler_params=_cparams("arbitrary", "arbitrary"),
        name="mlstm",
    )(qc, kc, p_ctx, p_ctx, *gates_ctx, ql, kl, p_lat, p_lat, *gates_lat, norm_g)


def _mix_ffn_kernel(a_ref, b_ref, gc_ref, gm_ref, hm_ref, x_ref, ga1_ref, sh2_ref, sc2_ref, ga2_ref,
                    dw_ref, lng_ref, lnb_ref, wc_ref, wm_ref, wo_ref, g2_ref, w1_ref, w2_ref, fg_ref,
                    o_ref, pad_ref, cv_ref, *, group, final_norm):
    tm = a_ref.shape[0]
    stride = group + 2 * CONV_HALO
    u = a_ref[...].astype(F32) * jax.nn.sigmoid(b_ref[...].astype(F32))
    u3 = pltpu.einshape("t(sl)->tsl", u, s=SUBLANES, l=LANES)
    halo = jnp.zeros((CONV_HALO, SUBLANES, LANES), F32)
    for g in range(tm // group):
        base = g * stride
        pad_ref[base:base + CONV_HALO] = halo
        pad_ref[base + CONV_HALO:base + CONV_HALO + group] = u3[g * group:(g + 1) * group]
        pad_ref[base + CONV_HALO + group:base + stride] = halo
    first = CONV_HALO - CONV_WIDTH // 2
    for g in range(tm // group):
        for tb in range(group // CONV_TOK_BLK):
            t0 = tb * CONV_TOK_BLK
            acc = jnp.zeros((CONV_TOK_BLK, SUBLANES, LANES), F32)
            for j in range(CONV_WIDTH):
                start = g * stride + first + j + t0
                acc = acc + pad_ref[start:start + CONV_TOK_BLK] * dw_ref[j]
            cv_ref[g * group + t0:g * group + t0 + CONV_TOK_BLK] = acc
    cv = pltpu.einshape("tsl->t(sl)", cv_ref[...])
    mu = jnp.mean(cv, axis=-1, keepdims=True)
    var = jnp.mean(jnp.square(cv - mu), axis=-1, keepdims=True)
    ln = (cv - mu) * lax.rsqrt(var + EPS) * lng_ref[...] + lnb_ref[...]
    y_conv = _bdot(_silu(ln).astype(BF16), wc_ref[...])
    y_m = _bdot(hm_ref[...], wm_ref[...])
    y = (jax.nn.sigmoid(gc_ref[...].astype(F32)) * y_conv
         + jax.nn.sigmoid(gm_ref[...].astype(F32)) * y_m)
    x1 = x_ref[...] + ga1_ref[...] * _bdot(y.astype(BF16), wo_ref[...])

    h2 = _rms_modulate(x1, g2_ref[...], sh2_ref[...], sc2_ref[...]).astype(BF16)
    acc = jnp.zeros(x1.shape, F32)
    for cb in range(D_FF // FF_BLK):
        cols = slice(cb * FF_BLK, (cb + 1) * FF_BLK)
        t = jnp.square(jnp.maximum(_bdot(h2, w1_ref[:, cols]), 0.0))
        acc = acc + _bdot(t.astype(BF16), w2_ref[cols, :])
    x2 = x1 + ga2_ref[...] * acc
    if final_norm:
        x2 = x2 * lax.rsqrt(jnp.mean(x2 * x2, axis=-1, keepdims=True) + EPS) * fg_ref[...]
    o_ref[...] = x2


def _mix_ffn(p, hm, x, mods, dw_w, ln_g, ln_b, w_conv, w_m, w_out, g2, w1, w2, final_g,
             row_fn, tm, group, final_norm):
    n = x.shape[0]
    pad_rows = tm // group * (group + 2 * CONV_HALO)
    const = lambda shape: pl.BlockSpec(shape, lambda i: (0, 0), pipeline_mode=pl.Buffered(1))
    pblk = lambda c: pl.BlockSpec((tm, D_MODEL), lambda i: (i, c))
    mod = lambda k: _mod_spec(k, row_fn)
    return pl.pallas_call(
        functools.partial(_mix_ffn_kernel, group=group, final_norm=final_norm),
        grid=(n // tm,),
        in_specs=[
            pblk(COL_A), pblk(COL_B), pblk(COL_GC), pblk(COL_GM),
            pl.BlockSpec((tm, MLSTM_DIM), lambda i: (i, 0)),
            pl.BlockSpec((tm, D_MODEL), lambda i: (i, 0)),
            mod(2), mod(3), mod(4), mod(5),
            pl.BlockSpec((CONV_WIDTH, SUBLANES, LANES), lambda i: (0, 0, 0),
                         pipeline_mode=pl.Buffered(1)),
            const((1, D_MODEL)), const((1, D_MODEL)),
            const((D_MODEL, D_MODEL)), const((MLSTM_DIM, D_MODEL)), const((D_MODEL, D_MODEL)),
            const((1, D_MODEL)), const((D_MODEL, D_FF)), const((D_FF, D_MODEL)), const((1, D_MODEL)),
        ],
        out_specs=pl.BlockSpec((tm, D_MODEL), lambda i: (i, 0)),
        out_shape=jax.ShapeDtypeStruct((n, D_MODEL), F32),
        scratch_shapes=[
            pltpu.VMEM((pad_rows, SUBLANES, LANES), F32),
            pltpu.VMEM((tm, SUBLANES, LANES), F32),
        ],
        compiler_params=_cparams("arbitrary"),
        name="mix_ffn",
    )(p, p, p, p, hm, x, mods, mods, mods, mods, dw_w, ln_g, ln_b, w_conv, w_m, w_out, g2, w1, w2,
      final_g)


def _gate_weights(w_g, b_g):
    w = w_g.reshape(D_MODEL, N_GATE, HEADS).transpose(0, 2, 1)
    b = b_g.reshape(N_GATE, HEADS).T
    pad_to = lambda a, width: jnp.pad(a, [(0, 0)] * (a.ndim - 1) + [(0, width - N_GATE)])
    wg_col = pad_to(w, LANES).reshape(D_MODEL, GATE_COLS).astype(BF16)
    wg_row = pad_to(w, SUBLANES).reshape(D_MODEL, GATE_ROWS).T.astype(BF16)
    return wg_col, wg_row, pad_to(b, LANES).reshape(1, GATE_COLS), pad_to(b, SUBLANES).reshape(GATE_ROWS, 1)


def _gate_views(g_col, g_row, bsz):
    n = g_col.shape[0]
    rows = g_row.reshape(HEADS, SUBLANES, n // CHUNK, CHUNK).transpose(0, 2, 1, 3)
    return rows, g_col.reshape(bsz, n // bsz, GATE_COLS)


def _token_tiles(t_lat, n_ctx):
    tm_proj, tm_mix = 1024, 512
    assert t_lat % tm_proj == 0 and n_ctx % tm_proj == 0
    assert t_lat % tm_mix == 0 and tm_mix % GRID_W == 0 and n_ctx % tm_mix == 0
    return tm_proj, tm_mix


def kernel(x, c, ctx, c_ctx, w_mod, b_mod, norm1_g, w_in, mlstm_gate_b, qk_conv_w, conv_dw_w,
           conv_ln_g, conv_ln_b, w_conv_out, mlstm_norm_g, w_mlstm_out, w_out, norm2_g,
           w_ff1, w_ff2, final_g):
    bsz, t_lat, _ = x.shape
    t_ctx = ctx.shape[1]
    depth = w_mod.shape[0]
    assert bsz < MOD_ROWS and t_lat % GRID_W == 0

    cc = jnp.zeros((MOD_ROWS, D_MODEL), F32).at[:bsz].set(c).at[bsz].set(c_ctx)
    mod_all = _modulation(cc, w_mod, b_mod)

    g_lo = N_MAIN_BLKS * W_BLK
    g_hi = g_lo + N_GATE * HEADS
    row2 = lambda a: a.reshape(1, -1)

    tm_proj, tm_mix = _token_tiles(t_lat, bsz * t_ctx)
    lat_row = lambda tm: (lambda i: i // (t_lat // tm))
    ctx_row = lambda i: bsz

    w_in16 = w_in.astype(BF16)
    xl = x.reshape(bsz * t_lat, D_MODEL)
    xc = ctx.reshape(bsz * t_ctx, D_MODEL)
    for l in range(depth):
        last = l == depth - 1
        mods = mod_all[l].reshape(MOD_ROWS, 1, N_MOD * D_MODEL)
        w_tail = w_in16[l, :, g_hi:]
        gate_ops = _gate_weights(w_in[l, :, g_lo:g_hi], mlstm_gate_b[l])
        w_conv, w_m, w_o = (w_conv_out[l].astype(BF16), w_mlstm_out[l].astype(BF16),
                            w_out[l].astype(BF16))
        w1, w2 = w_ff1[l].astype(BF16), w_ff2[l].astype(BF16)

        proj = functools.partial(_in_proj, mods=mods, g=row2(norm1_g[l]), w_in=w_in16, layer=l,
                                 w_tail=w_tail, wg_col=gate_ops[0], wg_row=gate_ops[1],
                                 gb_col=gate_ops[2], gb_row=gate_ops[3], tm=tm_proj)
        p_lat, gcol_lat, grow_lat = proj(xl, row_fn=lat_row(tm_proj))
        p_ctx, gcol_ctx, grow_ctx = proj(xc, row_fn=ctx_row)
        p_lat3 = p_lat.reshape(bsz, t_lat, P_DIM)
        p_ctx3 = p_ctx.reshape(bsz, t_ctx, P_DIM)
        ql, kl = _qk_conv(p_lat3, qk_conv_w[l])
        qc, kc = _qk_conv(p_ctx3, qk_conv_w[l])
        hm = _mlstm(qc, kc, p_ctx3, _gate_views(gcol_ctx, grow_ctx, bsz),
                    ql, kl, p_lat3, _gate_views(gcol_lat, grow_lat, bsz),
                    row2(mlstm_norm_g[l]), ctx_out=not last)

        mix_ffn = functools.partial(
            _mix_ffn, mods=mods, dw_w=conv_dw_w[l].reshape(CONV_WIDTH, SUBLANES, LANES),
            ln_g=row2(conv_ln_g[l]), ln_b=row2(conv_ln_b[l]),
            w_conv=w_conv, w_m=w_m, w_out=w_o, g2=row2(norm2_g[l]), w1=w1, w2=w2,
            final_g=row2(final_g), tm=tm_mix)
        xl = mix_ffn(p_lat, hm[0].reshape(bsz * t_lat, MLSTM_DIM), xl, row_fn=lat_row(tm_mix),
                     group=GRID_W, final_norm=last)
        if not last:
            xc = mix_ffn(p_ctx, hm[1].reshape(bsz * t_ctx, MLSTM_DIM), xc, row_fn=ctx_row,
                         group=t_ctx, final_norm=False)
    return xl.reshape(bsz, t_lat, D_MODEL)
```

```python
import functools

import jax
import jax.numpy as jnp
from jax import lax
from jax.experimental import pallas as pl
from jax.experimental.pallas import tpu as pltpu

D_MODEL = 1024
GRID_W = 64
CONV_WIDTH = 31
CONV_HALO = 16
CONV_TOK_BLK = 16
SUBLANES = 8
BF16_TILE_ROWS = 16
HEADS = 4
MLSTM_DIM = 2 * D_MODEL
HEAD_DIM = MLSTM_DIM // HEADS
QK_CONV_WIDTH = 5
CHUNK = 256
D_FF = 4 * D_MODEL
N_MOD = 6
EPS = 1e-6
M_INIT = -1e30
N_GATE = 4
LANES = 128
GATE_COLS = HEADS * LANES
GATE_ROWS = HEADS * SUBLANES
W_BLK = 2 * D_MODEL
N_MAIN_BLKS = 5
FF_BLK = D_MODEL
P_DIM = 12 * D_MODEL
COL_A, COL_B, COL_GC, COL_GM = 0, 1, 10, 11
COL_Q, COL_K, COL_V, COL_O = 4, 8, 12, 16
MOD_ROWS = 16
VMEM_LIMIT = 56 * 1024 * 1024

BF16 = jnp.bfloat16
F32 = jnp.float32


def _silu(x):
    return x * jax.nn.sigmoid(x)


def _bdot(a, b):
    return jnp.dot(a, b, preferred_element_type=F32)


def _cparams(*sem):
    return pltpu.CompilerParams(dimension_semantics=sem, vmem_limit_bytes=VMEM_LIMIT)


def _mod_kernel(cc_ref, w_ref, b_ref, o_ref):
    s = _silu(cc_ref[...]).astype(BF16)
    o_ref[...] = _bdot(s, w_ref[...].astype(BF16)) + b_ref[...]


def _modulation(cc, w_mod, b_mod):
    depth = w_mod.shape[0]
    tn = D_MODEL
    return pl.pallas_call(
        _mod_kernel,
        grid=(depth, N_MOD * D_MODEL // tn),
        in_specs=[
            pl.BlockSpec((MOD_ROWS, D_MODEL), lambda l, j: (0, 0)),
            pl.BlockSpec((None, D_MODEL, tn), lambda l, j: (l, 0, j)),
            pl.BlockSpec((None, 1, tn), lambda l, j: (l, 0, j)),
        ],
        out_specs=pl.BlockSpec((None, MOD_ROWS, tn), lambda l, j: (l, 0, j)),
        out_shape=jax.ShapeDtypeStruct((depth, MOD_ROWS, N_MOD * D_MODEL), F32),
        compiler_params=_cparams("arbitrary", "arbitrary"),
        name="modulation",
    )(cc, w_mod, b_mod.reshape(depth, 1, N_MOD * D_MODEL))


def _mod_spec(k, row_fn):
    return pl.BlockSpec((None, 1, D_MODEL), lambda *idx: (row_fn(*idx), 0, k))


def _rms_modulate(x, g, shift, scale):
    y = x * lax.rsqrt(jnp.mean(x * x, axis=-1, keepdims=True) + EPS) * g
    return y * (1.0 + scale) + shift


def _in_proj_kernel(x_ref, sh_ref, sc_ref, g_ref, w_ref, wt_ref, wgc_ref, wgr_ref, gbc_ref, gbr_ref,
                    p_ref, gcol_ref, grow_ref, h_scr):
    blk = pl.program_id(1)

    @pl.when(blk == 0)
    def _():
        h = _rms_modulate(x_ref[...], g_ref[...], sh_ref[...], sc_ref[...]).astype(BF16)
        h_scr[...] = h
        gcol_ref[...] = _bdot(h, wgc_ref[...]) + gbc_ref[...]
        grow_ref[...] = lax.dot_general(wgr_ref[...], h, (((1,), (1,)), ((), ())),
                                        preferred_element_type=F32) + gbr_ref[...]

    @pl.when(blk < N_MAIN_BLKS)
    def _():
        p_ref[...] = _bdot(h_scr[...], w_ref[...]).astype(BF16)

    @pl.when(blk == N_MAIN_BLKS)
    def _():
        p_ref[...] = _bdot(h_scr[...], wt_ref[...]).astype(BF16)


def _in_proj(x, mods, g, w_in, layer, w_tail, wg_col, wg_row, gb_col, gb_row, row_fn, tm):
    n = x.shape[0]
    const = lambda shape: pl.BlockSpec(shape, lambda i, j: (0, 0), pipeline_mode=pl.Buffered(1))
    return pl.pallas_call(
        _in_proj_kernel,
        grid=(n // tm, N_MAIN_BLKS + 1),
        in_specs=[
            pl.BlockSpec((tm, D_MODEL), lambda i, j: (i, 0)),
            _mod_spec(0, lambda i, j: row_fn(i)),
            _mod_spec(1, lambda i, j: row_fn(i)),
            const((1, D_MODEL)),
            pl.BlockSpec((None, D_MODEL, W_BLK),
                         lambda i, j: (layer, 0, jnp.minimum(j, N_MAIN_BLKS - 1))),
            const((D_MODEL, W_BLK)),
            const((D_MODEL, GATE_COLS)), const((GATE_ROWS, D_MODEL)),
            const((1, GATE_COLS)), const((GATE_ROWS, 1)),
        ],
        out_specs=[
            pl.BlockSpec((tm, W_BLK), lambda i, j: (i, j)),
            pl.BlockSpec((tm, GATE_COLS), lambda i, j: (i, 0)),
            pl.BlockSpec((GATE_ROWS, tm), lambda i, j: (0, i)),
        ],
        out_shape=[
            jax.ShapeDtypeStruct((n, P_DIM), BF16),
            jax.ShapeDtypeStruct((n, GATE_COLS), F32),
            jax.ShapeDtypeStruct((GATE_ROWS, n), F32),
        ],
        scratch_shapes=[pltpu.VMEM((tm, D_MODEL), BF16)],
        compiler_params=_cparams("arbitrary", "arbitrary"),
        name="in_proj",
    )(x, mods, mods, g, w_in, w_tail, wg_col, wg_row, gb_col, gb_row)


def _qk_conv_kernel(q_ref, k_ref, wq_ref, wk_ref, qo_ref, ko_ref):
    t = q_ref.shape[0]
    half = QK_CONV_WIDTH // 2
    edge = 2 * BF16_TILE_ROWS

    def taps(x, w_ref, valid_fn):
        n = x.shape[0]
        acc = x * w_ref[half:half + 1, :]
        for off in range(-half, half + 1):
            if off == 0:
                continue
            shifted = pltpu.roll(x, (-off) % n, 0)
            if valid_fn is not None:
                shifted = jnp.where(valid_fn(off), shifted, 0.0)
            acc = acc + shifted * w_ref[off + half:off + half + 1, :]
        return acc

    def conv(x_ref, w_ref, o_ref, scale):
        def out(acc):
            y = _silu(acc)
            return (y if scale is None else y * scale).astype(BF16)

        o_ref[...] = out(taps(x_ref[...].astype(F32), w_ref, None))
        row = lax.broadcasted_iota(jnp.int32, (edge, 1), 0)
        top = taps(x_ref[0:edge, :].astype(F32), w_ref, lambda off: row + off >= 0)
        o_ref[0:BF16_TILE_ROWS, :] = out(top[0:BF16_TILE_ROWS, :])
        bot = taps(x_ref[t - edge:t, :].astype(F32), w_ref, lambda off: row + off < edge)
        o_ref[t - BF16_TILE_ROWS:t, :] = out(bot[edge - BF16_TILE_ROWS:edge, :])

    conv(q_ref, wq_ref, qo_ref, None)
    conv(k_ref, wk_ref, ko_ref, HEAD_DIM ** -0.5)


def _qk_conv(p, qk_w):
    bsz, t, _ = p.shape
    out = jax.ShapeDtypeStruct((bsz, t, MLSTM_DIM), BF16)
    blk = lambda c0: pl.BlockSpec((None, t, HEAD_DIM), lambda b, c: (b, 0, c0 + c))
    wblk = lambda c0: pl.BlockSpec((QK_CONV_WIDTH, HEAD_DIM), lambda b, c: (0, c0 + c))
    return pl.pallas_call(
        _qk_conv_kernel,
        grid=(bsz, HEADS),
        in_specs=[blk(COL_Q), blk(COL_K), wblk(0), wblk(HEADS)],
        out_specs=[blk(0), blk(0)],
        out_shape=[out, out],
        compiler_params=_cparams("arbitrary", "arbitrary"),
        name="qk_conv",
    )(p, p, qk_w, qk_w)


def _mlstm_chunk(q, k, v, g_row, g_col, state, reverse):
    c_ref, c16_ref, n_ref, m_ref = state
    gi, gf = (2, 3) if reverse else (0, 1)
    i_row, i_col = g_row[gi:gi + 1, :], g_col[:, gi:gi + 1]
    lf_row = jax.nn.log_sigmoid(g_row[gf:gf + 1, :])
    lf_col = jax.nn.log_sigmoid(g_col[:, gf:gf + 1])
    r = lax.broadcasted_iota(jnp.int32, (CHUNK, CHUNK), 0)
    c = lax.broadcasted_iota(jnp.int32, (CHUNK, CHUNK), 1)
    mask = (c >= r) if reverse else (c <= r)
    mask_t = (r >= c) if reverse else (r <= c)
    b_col = jnp.sum(jnp.where(mask, lf_row, 0.0), axis=1, keepdims=True)
    b_row = jnp.sum(jnp.where(mask_t, lf_col, 0.0), axis=0, keepdims=True)
    b_last = jnp.sum(lf_row, axis=1, keepdims=True)
    m_st = m_ref[...]

    log_d = jnp.where(mask, b_col - b_row + i_row, -jnp.inf)
    inter = b_col + m_st
    m_t = jnp.maximum(inter, jnp.max(log_d, axis=1, keepdims=True))
    w_intra = jnp.exp(log_d - m_t)
    w_inter = jnp.exp(inter - m_t)
    s = lax.dot_general(q, k, (((1,), (1,)), ((), ())), preferred_element_type=F32) * w_intra
    num = w_inter * _bdot(q, c16_ref[...]) + _bdot(s.astype(BF16), v)
    n16 = jnp.broadcast_to(n_ref[...], (SUBLANES, HEAD_DIM)).astype(BF16)
    qn = lax.dot_general(q, n16, (((1,), (1,)), ((), ())), preferred_element_type=F32)[:, 0:1]
    den = w_inter * qn + jnp.sum(s, axis=1, keepdims=True)
    h = num / jnp.maximum(jnp.abs(den), jnp.exp(-m_t))

    g_end = b_last - b_col + i_col
    inter_end = b_last + m_st
    m_new = jnp.maximum(inter_end, jnp.max(g_end, axis=0, keepdims=True))
    ws = jnp.exp(g_end - m_new)
    we = jnp.exp(inter_end - m_new)
    kw = k.astype(F32) * ws
    kv = lax.dot_general(kw.astype(BF16), v, (((0,), (0,)), ((), ())), preferred_element_type=F32)
    c_new = we * c_ref[...] + kv
    c_ref[...] = c_new
    c16_ref[...] = c_new.astype(BF16)
    n_ref[...] = we * n_ref[...] + jnp.sum(kw, axis=0, keepdims=True)
    m_ref[...] = m_new
    return h


def _mlstm_kernel(qc_ref, kc_ref, vc_ref, oc_ref, growc_ref, gcolc_ref,
                  ql_ref, kl_ref, vl_ref, ol_ref, growl_ref, gcoll_ref, ng_ref, *rest, ctx_out):
    if ctx_out:
        hl_ref, hc_ref = rest[:2]
        scratch = rest[2:]
        accc_ref = scratch[9]
    else:
        hl_ref, hc_ref, accc_ref = rest[0], None, None
        scratch = rest[1:]
    state_f, state_b, accl_ref = scratch[0:4], scratch[4:8], scratch[8]
    nc_ctx = qc_ref.shape[0] // CHUNK
    nc_lat = ql_ref.shape[0] // CHUNK
    assert (nc_ctx == 1 or nc_ctx % 2 == 0) and nc_lat % 2 == 0

    for c_ref, c16_ref, n_ref, m_ref in (state_f, state_b):
        c_ref[...] = jnp.zeros_like(c_ref)
        c16_ref[...] = jnp.zeros_like(c16_ref)
        n_ref[...] = jnp.zeros_like(n_ref)
        m_ref[...] = jnp.full_like(m_ref, M_INIT)

    def finish(h, o_ref, out_ref, rows):
        hn = h * lax.rsqrt(jnp.mean(h * h, axis=-1, keepdims=True) + EPS) * ng_ref[...]
        out_ref[rows, :] = (hn * jax.nn.sigmoid(o_ref[rows, :].astype(F32))).astype(BF16)

    def pair_step(i, n_chunks, refs, o_ref, out_ref, acc_ref, closing):
        q_ref, k_ref, v_ref, grow_ref, gcol_ref = refs
        steps = ((i, state_f, False, closing[0]), (n_chunks - 1 - i, state_b, True, closing[1]))
        for j, state, reverse, close in steps:
            if isinstance(j, int):
                rows = pl.ds(j * CHUNK, CHUNK)
            else:
                rows = pl.ds(pl.multiple_of(j * CHUNK, CHUNK), CHUNK)
            h = _mlstm_chunk(q_ref[rows, :], k_ref[rows, :], v_ref[rows, :],
                             grow_ref[j], gcol_ref[rows, :], state, reverse)
            if out_ref is None:
                continue
            if close:
                finish(acc_ref[rows, :] + h, o_ref, out_ref, rows)
            else:
                acc_ref[rows, :] = h

    ctx = ((qc_ref, kc_ref, vc_ref, growc_ref, gcolc_ref), oc_ref, hc_ref, accc_ref)
    lat = ((ql_ref, kl_ref, vl_ref, growl_ref, gcoll_ref), ol_ref, hl_ref, accl_ref)
    for n_chunks, args in ((nc_ctx, ctx), (nc_lat, lat)):
        half = n_chunks // 2

        def opening(i, carry, n_chunks=n_chunks, args=args):
            pair_step(i, n_chunks, *args, closing=(False, False))
            return carry

        def closing(i, carry, n_chunks=n_chunks, args=args):
            pair_step(i, n_chunks, *args, closing=(True, True))
            return carry

        if n_chunks == 1:
            pair_step(0, 1, *args, closing=(False, True))
        elif half == 1:
            opening(0, 0)
            closing(1, 0)
        else:
            lax.fori_loop(0, half, opening, 0)
            lax.fori_loop(half, n_chunks, closing, 0)


def _mlstm(qc, kc, p_ctx, gates_ctx, ql, kl, p_lat, gates_lat, norm_g, ctx_out):
    bsz, t_ctx, _ = qc.shape
    t_lat = ql.shape[1]
    blk = lambda t, c0: pl.BlockSpec((None, t, HEAD_DIM), lambda b, h: (b, 0, c0 + h))

    def stream_specs(t):
        single = pl.Buffered(1)
        return [
            blk(t, 0), blk(t, 0), blk(t, COL_V),
            pl.BlockSpec((None, t, HEAD_DIM), lambda b, h: (b, 0, COL_O + h), pipeline_mode=single),
            pl.BlockSpec((None, t // CHUNK, SUBLANES, CHUNK), lambda b, h: (h, b, 0, 0)),
            pl.BlockSpec((None, t, LANES), lambda b, h: (b, 0, h), pipeline_mode=single),
        ]

    in_specs = (stream_specs(t_ctx) + stream_specs(t_lat)
                + [pl.BlockSpec((1, HEAD_DIM), lambda b, h: (0, h))])
    out_specs = [blk(t_lat, 0)]
    out_shape = [jax.ShapeDtypeStruct((bsz, t_lat, MLSTM_DIM), BF16)]
    state = [pltpu.VMEM((HEAD_DIM, HEAD_DIM), F32), pltpu.VMEM((HEAD_DIM, HEAD_DIM), BF16),
             pltpu.VMEM((1, HEAD_DIM), F32), pltpu.VMEM((1, 1), F32)]
    scratch = state + state + [pltpu.VMEM((t_lat, HEAD_DIM), F32)]
    if ctx_out:
        out_specs.append(blk(t_ctx, 0))
        out_shape.append(jax.ShapeDtypeStruct((bsz, t_ctx, MLSTM_DIM), BF16))
        scratch.append(pltpu.VMEM((t_ctx, HEAD_DIM), F32))
    return pl.pallas_call(
        functools.partial(_mlstm_kernel, ctx_out=ctx_out),
        grid=(bsz, HEADS),
        in_specs=in_specs,
        out_specs=out_specs,
        out_shape=out_shape,
        scratch_shapes=scratch,
        compiler_params=_cparams("arbitrary", "arbitrary"),
        name="mlstm",
    )(qc, kc, p_ctx, p_ctx, *gates_ctx, ql, kl, p_lat, p_lat, *gates_lat, norm_g)


def _mix_ffn_kernel(a_ref, b_ref, gc_ref, gm_ref, hm_ref, x_ref, ga1_ref, sh2_ref, sc2_ref, ga2_ref,
                    dw_ref, lng_ref, lnb_ref, wc_ref, wm_ref, wo_ref, g2_ref, w1_ref, w2_ref, fg_ref,
                    o_ref, pad_ref, cv_ref, *, group, final_norm):
    tm = a_ref.shape[0]
    stride = group + 2 * CONV_HALO
    u = a_ref[...].astype(F32) * jax.nn.sigmoid(b_ref[...].astype(F32))
    u3 = pltpu.einshape("t(sl)->tsl", u, s=SUBLANES, l=LANES)
    halo = jnp.zeros((CONV_HALO, SUBLANES, LANES), F32)
    for g in range(tm // group):
        base = g * stride
        pad_ref[base:base + CONV_HALO] = halo
        pad_ref[base + CONV_HALO:base + CONV_HALO + group] = u3[g * group:(g + 1) * group]
        pad_ref[base + CONV_HALO + group:base + stride] = halo
    first = CONV_HALO - CONV_WIDTH // 2
    for g in range(tm // group):
        for tb in range(group // CONV_TOK_BLK):
            t0 = tb * CONV_TOK_BLK
            acc = jnp.zeros((CONV_TOK_BLK, SUBLANES, LANES), F32)
            for j in range(CONV_WIDTH):
                start = g * stride + first + j + t0
                acc = acc + pad_ref[start:start + CONV_TOK_BLK] * dw_ref[j]
            cv_ref[g * group + t0:g * group + t0 + CONV_TOK_BLK] = acc
    cv = pltpu.einshape("tsl->t(sl)", cv_ref[...])
    mu = jnp.mean(cv, axis=-1, keepdims=True)
    var = jnp.mean(jnp.square(cv - mu), axis=-1, keepdims=True)
    ln = (cv - mu) * lax.rsqrt(var + EPS) * lng_ref[...] + lnb_ref[...]
    y_conv = _bdot(_silu(ln).astype(BF16), wc_ref[...])
    y_m = _bdot(hm_ref[...], wm_ref[...])
    y = (jax.nn.sigmoid(gc_ref[...].astype(F32)) * y_conv
         + jax.nn.sigmoid(gm_ref[...].astype(F32)) * y_m)
    x1 = x_ref[...] + ga1_ref[...] * _bdot(y.astype(BF16), wo_ref[...])

    h2 = _rms_modulate(x1, g2_ref[...], sh2_ref[...], sc2_ref[...]).astype(BF16)
    acc = jnp.zeros(x1.shape, F32)
    for cb in range(D_FF // FF_BLK):
        cols = slice(cb * FF_BLK, (cb + 1) * FF_BLK)
        t = jnp.square(jnp.maximum(_bdot(h2, w1_ref[:, cols]), 0.0))
        acc = acc + _bdot(t.astype(BF16), w2_ref[cols, :])
    x2 = x1 + ga2_ref[...] * acc
    if final_norm:
        x2 = x2 * lax.rsqrt(jnp.mean(x2 * x2, axis=-1, keepdims=True) + EPS) * fg_ref[...]
    o_ref[...] = x2


def _mix_ffn(p, hm, x, mods, dw_w, ln_g, ln_b, w_conv, w_m, w_out, g2, w1, w2, final_g,
             row_fn, tm, group, final_norm):
    n = x.shape[0]
    pad_rows = tm // group * (group + 2 * CONV_HALO)
    const = lambda shape: pl.BlockSpec(shape, lambda i: (0, 0), pipeline_mode=pl.Buffered(1))
    pblk = lambda c: pl.BlockSpec((tm, D_MODEL), lambda i: (i, c))
    mod = lambda k: _mod_spec(k, row_fn)
    return pl.pallas_call(
        functools.partial(_mix_ffn_kernel, group=group, final_norm=final_norm),
        grid=(n // tm,),
        in_specs=[
            pblk(COL_A), pblk(COL_B), pblk(COL_GC), pblk(COL_GM),
            pl.BlockSpec((tm, MLSTM_DIM), lambda i: (i, 0)),
            pl.BlockSpec((tm, D_MODEL), lambda i: (i, 0)),
            mod(2), mod(3), mod(4), mod(5),
            pl.BlockSpec((CONV_WIDTH, SUBLANES, LANES), lambda i: (0, 0, 0),
                         pipeline_mode=pl.Buffered(1)),
            const((1, D_MODEL)), const((1, D_MODEL)),
            const((D_MODEL, D_MODEL)), const((MLSTM_DIM, D_MODEL)), const((D_MODEL, D_MODEL)),
            const((1, D_MODEL)), const((D_MODEL, D_FF)), const((D_FF, D_MODEL)), const((1, D_MODEL)),
        ],
        out_specs=pl.BlockSpec((tm, D_MODEL), lambda i: (i, 0)),
        out_shape=jax.ShapeDtypeStruct((n, D_MODEL), F32),
        scratch_shapes=[
            pltpu.VMEM((pad_rows, SUBLANES, LANES), F32),
            pltpu.VMEM((tm, SUBLANES, LANES), F32),
        ],
        compiler_params=_cparams("arbitrary"),
        name="mix_ffn",
    )(p, p, p, p, hm, x, mods, mods, mods, mods, dw_w, ln_g, ln_b, w_conv, w_m, w_out, g2, w1, w2,
      final_g)


def _gate_weights(w_g, b_g):
    w = w_g.reshape(D_MODEL, N_GATE, HEADS).transpose(0, 2, 1)
    b = b_g.reshape(N_GATE, HEADS).T
    pad_to = lambda a, width: jnp.pad(a, [(0, 0)] * (a.ndim - 1) + [(0, width - N_GATE)])
    wg_col = pad_to(w, LANES).reshape(D_MODEL, GATE_COLS).astype(BF16)
    wg_row = pad_to(w, SUBLANES).reshape(D_MODEL, GATE_ROWS).T.astype(BF16)
    return wg_col, wg_row, pad_to(b, LANES).reshape(1, GATE_COLS), pad_to(b, SUBLANES).reshape(GATE_ROWS, 1)


def _gate_views(g_col, g_row, bsz):
    n = g_col.shape[0]
    rows = g_row.reshape(HEADS, SUBLANES, n // CHUNK, CHUNK).transpose(0, 2, 1, 3)
    return rows, g_col.reshape(bsz, n // bsz, GATE_COLS)


def _token_tiles(t_lat, n_ctx):
    tm_proj, tm_mix = 1024, 512
    assert t_lat % tm_proj == 0 and n_ctx % tm_proj == 0
    assert t_lat % tm_mix == 0 and tm_mix % GRID_W == 0 and n_ctx % tm_mix == 0
    return tm_proj, tm_mix


def kernel(x, c, ctx, c_ctx, w_mod, b_mod, norm1_g, w_in, mlstm_gate_b, qk_conv_w, conv_dw_w,
           conv_ln_g, conv_ln_b, w_conv_out, mlstm_norm_g, w_mlstm_out, w_out, norm2_g,
           w_ff1, w_ff2, final_g):
    bsz, t_lat, _ = x.shape
    t_ctx = ctx.shape[1]
    depth = w_mod.shape[0]
    assert bsz < MOD_ROWS and t_lat % GRID_W == 0

    cc = jnp.zeros((MOD_ROWS, D_MODEL), F32).at[:bsz].set(c).at[bsz].set(c_ctx)
    mod_all = _modulation(cc, w_mod, b_mod)

    g_lo = N_MAIN_BLKS * W_BLK
    g_hi = g_lo + N_GATE * HEADS
    row2 = lambda a: a.reshape(1, -1)

    tm_proj, tm_mix = _token_tiles(t_lat, bsz * t_ctx)
    lat_row = lambda tm: (lambda i: i // (t_lat // tm))
    ctx_row = lambda i: bsz

    w_in16 = w_in.astype(BF16)
    xl = x.reshape(bsz * t_lat, D_MODEL)
    xc = ctx.reshape(bsz * t_ctx, D_MODEL)
    for l in range(depth):
        last = l == depth - 1
        mods = mod_all[l].reshape(MOD_ROWS, 1, N_MOD * D_MODEL)
        w_tail = w_in16[l, :, g_hi:]
        gate_ops = _gate_weights(w_in[l, :, g_lo:g_hi], mlstm_gate_b[l])
        w_conv, w_m, w_o = (w_conv_out[l].astype(BF16), w_mlstm_out[l].astype(BF16),
                            w_out[l].astype(BF16))
        w1, w2 = w_ff1[l].astype(BF16), w_ff2[l].astype(BF16)

        proj = functools.partial(_in_proj, mods=mods, g=row2(norm1_g[l]), w_in=w_in16, layer=l,
                                 w_tail=w_tail, wg_col=gate_ops[0], wg_row=gate_ops[1],
                                 gb_col=gate_ops[2], gb_row=gate_ops[3], tm=tm_proj)
        p_lat, gcol_lat, grow_lat = proj(xl, row_fn=lat_row(tm_proj))
        p_ctx, gcol_ctx, grow_ctx = proj(xc, row_fn=ctx_row)
        p_lat3 = p_lat.reshape(bsz, t_lat, P_DIM)
        p_ctx3 = p_ctx.reshape(bsz, t_ctx, P_DIM)
        ql, kl = _qk_conv(p_lat3, qk_conv_w[l])
        qc, kc = _qk_conv(p_ctx3, qk_conv_w[l])
        hm = _mlstm(qc, kc, p_ctx3, _gate_views(gcol_ctx, grow_ctx, bsz),
                    ql, kl, p_lat3, _gate_views(gcol_lat, grow_lat, bsz),
                    row2(mlstm_norm_g[l]), ctx_out=not last)

        mix_ffn = functools.partial(
            _mix_ffn, mods=mods, dw_w=conv_dw_w[l].reshape(CONV_WIDTH, SUBLANES, LANES),
            ln_g=row2(conv_ln_g[l]), ln_b=row2(conv_ln_b[l]),
            w_conv=w_conv, w_m=w_m, w_out=w_o, g2=row2(norm2_g[l]), w1=w1, w2=w2,
            final_g=row2(final_g), tm=tm_mix)
        xl = mix_ffn(p_lat, hm[0].reshape(bsz * t_lat, MLSTM_DIM), xl, row_fn=lat_row(tm_mix),
                     group=GRID_W, final_norm=last)
        if not last:
            xc = mix_ffn(p_ctx, hm[1].reshape(bsz * t_ctx, MLSTM_DIM), xc, row_fn=ctx_row,
                         group=t_ctx, final_norm=False)
    return xl.reshape(bsz, t_lat, D_MODEL)
```
